```python
import math
import jax, jax.numpy as jnp
from jax import lax
import numpy as np

D_MODEL = 2048
BATCH = 2
SEQ = 4096
DEPTH = 4
DEC_BATCH = 8
DEC_SEQ = 1
PAST_LEN = 16384
PAGE_SIZE = 128

HEAD_DIM = 128
BRANCH_WIDTH = D_MODEL // 2
N_BRANCH = 3
A_HEADS = BRANCH_WIDTH // HEAD_DIM
A_KV_HEADS = 2
IDX_HEADS = 16
IDX_DIM = 64
TOPK_MAX = 256
B_V_DIM = 2 * HEAD_DIM
B_HEADS = BRANCH_WIDTH // B_V_DIM
C_HEADS = 4
C_V_DIM = BRANCH_WIDTH // C_HEADS
C_K_DIM = BRANCH_WIDTH // 2 // C_HEADS
GLA_RANK = 16
GLA_TAU = 16.0
GLA_CHUNK = 32
D_FF = 4 * D_MODEL
Q_BLOCK = 128
ROPE_THETA = 10000.0
LN_EPS = 1e-5
RMS_EPS = 1e-6
DEEPNORM_ALPHA = (2.0 * DEPTH) ** 0.25
DEEPNORM_BETA = (8.0 * DEPTH) ** -0.25

SEGMENTS = (
    ('qa', A_HEADS * HEAD_DIM), ('ka', A_KV_HEADS * HEAD_DIM), ('va', A_KV_HEADS * HEAD_DIM),
    ('qi', IDX_HEADS * IDX_DIM), ('ki', IDX_DIM), ('wi', IDX_HEADS),
    ('qb', B_HEADS * 2 * HEAD_DIM), ('kb', B_HEADS * 2 * HEAD_DIM), ('vb', B_HEADS * B_V_DIM),
    ('qc', C_HEADS * C_K_DIM), ('kc', C_HEADS * C_K_DIM), ('vc', C_HEADS * C_V_DIM),
    ('gc', GLA_RANK), ('rc', C_HEADS * C_V_DIM),
    ('gate', N_BRANCH * D_MODEL),
)
VALUE_SEGMENTS = ('va', 'vb', 'vc')
W_IN_COLS = sum(w for _, w in SEGMENTS)

kernel_name = 'hybrid_dsa_diff_gla_decoder_step'

F32 = jnp.float32


def split_cols(z):
    out = {}
    off = 0
    for name, width in SEGMENTS:
        out[name] = z[..., off:off + width]
        off += width
    return out


def rope(x, pos):
    half = x.shape[-1] // 2
    inv_freq = ROPE_THETA ** (-jnp.arange(half, dtype=F32) / half)
    ang = pos.astype(F32)[:, None] * inv_freq[None, :]
    ang = ang.reshape((ang.shape[0],) + (1,) * (x.ndim - 3) + (half,))
    cos, sin = jnp.cos(ang), jnp.sin(ang)
    xf = x.astype(F32)
    x1, x2 = xf[..., :half], xf[..., half:]
    return jnp.concatenate([x1 * cos - x2 * sin, x2 * cos + x1 * sin], axis=-1).astype(x.dtype)


def layernorm(x, g, b):
    xf = x.astype(F32)
    mu = jnp.mean(xf, axis=-1, keepdims=True)
    var = jnp.mean(jnp.square(xf - mu), axis=-1, keepdims=True)
    return ((xf - mu) * lax.rsqrt(var + LN_EPS) * g.astype(F32) + b.astype(F32)).astype(x.dtype)


def rmsnorm(x, g):
    xf = x.astype(F32)
    return xf * lax.rsqrt(jnp.mean(jnp.square(xf), axis=-1, keepdims=True) + RMS_EPS) * g.astype(F32)


def to_blocks(a, nb):
    return jnp.moveaxis(a.reshape((a.shape[0], nb, Q_BLOCK) + a.shape[2:]), 1, 0)


take_rows = jax.vmap(lambda rows, idx: rows[idx])


def project(x, pos, lw):
    Bn, T, _ = x.shape
    z = split_cols(jnp.einsum('btd,de->bte', x, lw['w_in']))

    def r(name, *shape):
        return z[name].reshape((Bn, T) + shape)

    gc = jnp.einsum('btr,re->bte', z['gc'], lw['gla_w2']) + lw['gla_b']
    return {
        'qa': rope(r('qa', A_HEADS, HEAD_DIM), pos),
        'ka': rope(r('ka', A_KV_HEADS, HEAD_DIM), pos),
        'va': r('va', A_KV_HEADS, HEAD_DIM),
        'qi': rope(r('qi', IDX_HEADS, IDX_DIM), pos) * IDX_DIM ** -0.5,
        'ki': rope(z['ki'], pos),
        'wi': z['wi'] * IDX_HEADS ** -0.5,
        'qb': rope(r('qb', B_HEADS, 2, HEAD_DIM), pos),
        'kb': rope(r('kb', B_HEADS, 2, HEAD_DIM), pos),
        'vb': r('vb', B_HEADS, B_V_DIM),
        'qc': r('qc', C_HEADS, C_K_DIM) * C_K_DIM ** -0.5,
        'kc': r('kc', C_HEADS, C_K_DIM),
        'vc': r('vc', C_HEADS, C_V_DIM),
        'log_a': (jax.nn.log_sigmoid(gc.astype(F32)) / GLA_TAU).reshape(Bn, T, C_HEADS, C_K_DIM),
        'rc': z['rc'],
        'gates': jax.nn.sigmoid(r('gate', N_BRANCH, D_MODEL)),
    }


def indexer_scores(qi, wi, ki):
    s = jnp.einsum('bqhd,bsd->bqhs', qi, ki).astype(F32)
    return jnp.einsum('bqh,bqhs->bqs', wi.astype(F32), jax.nn.relu(s))


def gathered_gqa(qa, kg, vg, valid):
    Bn, Q = qa.shape[:2]
    q = qa.reshape(Bn, Q, A_KV_HEADS, A_HEADS // A_KV_HEADS, HEAD_DIM)
    s = jnp.einsum('bqgrd,bqkgd->bqgrk', q, kg).astype(F32) * HEAD_DIM ** -0.5
    s = jnp.where(valid[:, :, None, None, :], s, -jnp.inf)
    p = jax.nn.softmax(s, axis=-1).astype(vg.dtype)
    o = jnp.einsum('bqgrk,bqkgd->bqgrd', p, vg)
    return o.reshape(Bn, Q, A_HEADS * HEAD_DIM)


def dsa_prompt(qa, ka, va, qi, ki, wi):
    Bn, T = qa.shape[:2]
    top_k = min(TOPK_MAX, T // 4)
    nb = T // Q_BLOCK
    key_pos = jnp.arange(T)

    def one_block(args):
        qa_b, qi_b, wi_b, start = args
        q_pos = start + jnp.arange(Q_BLOCK)
        score = indexer_scores(qi_b, wi_b, ki)
        score = jnp.where((key_pos[None, :] <= q_pos[:, None])[None], score, -jnp.inf)
        _, idx = lax.top_k(score, top_k)
        valid = idx <= q_pos[None, :, None]
        return gathered_gqa(qa_b, take_rows(ka, idx), take_rows(va, idx), valid)

    starts = jnp.arange(nb) * Q_BLOCK
    o = lax.map(one_block, (to_blocks(qa, nb), to_blocks(qi, nb), to_blocks(wi, nb), starts))
    return jnp.moveaxis(o, 0, 1).reshape(Bn, T, A_HEADS * HEAD_DIM)


def dsa_sample(qa, ka, va, qi, ki, wi, pool_k, pool_v, pool_ki, page_table, layer):
    DB, Tn = qa.shape[:2]
    P = page_table.shape[1] * PAGE_SIZE
    L = P + Tn
    top_k = min(TOPK_MAX, L // 4)
    ki_past = pool_ki[layer, page_table].reshape(DB, P, IDX_DIM)
    ki_all = jnp.concatenate([ki_past, ki.astype(ki_past.dtype)], axis=1)
    q_pos = P + jnp.arange(Tn)
    score = indexer_scores(qi, wi, ki_all)
    score = jnp.where((jnp.arange(L)[None, :] <= q_pos[:, None])[None], score, -jnp.inf)
    _, idx = lax.top_k(score, top_k)
    valid = idx <= q_pos[None, :, None]
    in_past = idx < P
    sp = jnp.minimum(idx, P - 1)
    phys = take_rows(page_table, sp // PAGE_SIZE)
    off = sp % PAGE_SIZE
    sn = jnp.clip(idx - P, 0, Tn - 1)
    sel = in_past[..., None, None]
    kg = jnp.where(sel, pool_k[layer, phys, off], take_rows(ka, sn).astype(pool_k.dtype))
    vg = jnp.where(sel, pool_v[layer, phys, off], take_rows(va, sn).astype(pool_v.dtype))
    return gathered_gqa(qa.astype(kg.dtype), kg, vg, valid)


def diff_attn_prompt(qb, kb, vb):
    Bn, T = qb.shape[:2]
    nb = T // Q_BLOCK
    key_pos = jnp.arange(T)

    def one_block(args):
        q_b, start = args
        q_pos = start + jnp.arange(Q_BLOCK)
        s = jnp.einsum('bqhmd,bshmd->bhmqs', q_b, kb).astype(F32) * HEAD_DIM ** -0.5
        s = jnp.where(key_pos[None, :] <= q_pos[:, None], s, -jnp.inf)
        p = jax.nn.softmax(s, axis=-1).astype(vb.dtype)
        return jnp.einsum('bhmqs,bshe->bqhme', p, vb)

    starts = jnp.arange(nb) * Q_BLOCK
    o = lax.map(one_block, (to_blocks(qb, nb), starts))
    return jnp.moveaxis(o, 0, 1).reshape(Bn, T, B_HEADS, 2, B_V_DIM)


def softmax_accumulate(carry, s, v):
    m, den, acc = carry
    m_new = jnp.maximum(m, jnp.max(s, axis=-1))
    corr = jnp.exp(m - m_new)
    p = jnp.exp(s - m_new[..., None])
    acc = acc * corr[..., None] + jnp.einsum('bhmqs,bshe->bhmqe', p, v.astype(F32))
    return (m_new, den * corr + jnp.sum(p, axis=-1), acc)


def diff_attn_sample(qb, kb, vb, pool_k, pool_v, page_table, layer):
    DB, Tn = qb.shape[:2]
    q = qb.astype(F32) * HEAD_DIM ** -0.5
    init = (jnp.full((DB, B_HEADS, 2, Tn), -jnp.inf, F32),
            jnp.zeros((DB, B_HEADS, 2, Tn), F32),
            jnp.zeros((DB, B_HEADS, 2, Tn, B_V_DIM), F32))

    def page_step(carry, phys):
        kp = pool_k[layer, phys]
        vp = pool_v[layer, phys]
        s = jnp.einsum('bqhmd,bshmd->bhmqs', q, kp.astype(F32))
        return softmax_accumulate(carry, s, vp), None

    carry, _ = lax.scan(page_step, init, page_table.T)
    s_new = jnp.einsum('bqhmd,bshmd->bhmqs', q, kb.astype(F32))
    causal = jnp.arange(Tn)[None, :] <= jnp.arange(Tn)[:, None]
    s_new = jnp.where(causal, s_new, -jnp.inf)
    _, den, acc = softmax_accumulate(carry, s_new, vb)
    o = acc / den[..., None]
    return jnp.transpose(o, (0, 3, 1, 2, 4))


def diff_finish(o2, lw, layer):
    lam_init = 0.8 - 0.6 * math.exp(-0.3 * layer)
    lam = (jnp.exp(jnp.sum(lw['lam_q1'].astype(F32) * lw['lam_k1'].astype(F32)))
           - jnp.exp(jnp.sum(lw['lam_q2'].astype(F32) * lw['lam_k2'].astype(F32))) + lam_init)
    a = o2[..., 0, :].astype(F32) - lam * o2[..., 1, :].astype(F32)
    a = rmsnorm(a, lw['diff_subln']) * (1.0 - lam_init)
    return a.reshape(a.shape[0], a.shape[1], B_HEADS * B_V_DIM)


def gla_chunked(q, k, v, log_a, S0):
    Bn, T, H, DK = q.shape
    DV = v.shape[-1]
    C = math.gcd(T, GLA_CHUNK)
    n = T // C

    def chunks(a):
        return jnp.transpose(a.astype(F32).reshape(Bn, n, C, H, a.shape[-1]), (1, 0, 3, 2, 4))

    causal = jnp.tril(jnp.ones((C, C), dtype=bool))

    def step(S, xs):
        qc, kc, vc, gc = xs
        b = jnp.cumsum(gc, axis=2)
        qe = qc * jnp.exp(b)
        ke = kc * jnp.exp(-b)
        att = jnp.where(causal, jnp.einsum('bhcd,bhsd->bhcs', qe, ke), 0.0)
        o = jnp.einsum('bhcd,bhde->bhce', qe, S) + jnp.einsum('bhcs,bhse->bhce', att, vc)
        b_last = b[:, :, -1:, :]
        S = (jnp.exp(b_last[:, :, 0, :])[..., None] * S
             + jnp.einsum('bhsd,bhse->bhde', kc * jnp.exp(b_last - b), vc))
        return S, o

    S_fin, o = lax.scan(step, S0.astype(F32), (chunks(q), chunks(k), chunks(v), chunks(log_a)))
    o = jnp.transpose(o, (1, 0, 3, 2, 4)).reshape(Bn, T, H, DV)
    return o, S_fin


def layer_tail(x, oA, oB2, oC, pr, lw, layer):
    dt = x.dtype
    oB = diff_finish(oB2, lw, layer)
    oCn = rmsnorm(oC, lw['gla_norm']).reshape(oC.shape[0], oC.shape[1], C_HEADS * C_V_DIM)
    oCn = oCn * jax.nn.silu(pr['rc'].astype(F32))
    br = jnp.stack([oA.astype(dt), oB.astype(dt), oCn.astype(dt)], axis=2)
    u = jnp.einsum('btnc,ncd->btnd', br, lw['w_branch'])
    h = jnp.einsum('btd,de->bte', jnp.sum(pr['gates'] * u, axis=2), lw['w_out'])
    x = layernorm(DEEPNORM_ALPHA * x + h, lw['ln1_g'], lw['ln1_b'])
    hid = jnp.square(jax.nn.relu(jnp.einsum('btd,df->btf', x, lw['w_up'])))
    f = jnp.einsum('btf,fd->btd', hid, lw['w_down'])
    return layernorm(DEEPNORM_ALPHA * x + f, lw['ln2_g'], lw['ln2_b'])


def setup_inputs(seed: int = 0) -> dict:
    key = jax.random.key(seed)
    ks = jax.random.split(key, 32)
    n_pages = PAST_LEN // PAGE_SIZE
    n_phys = (5 * DEC_BATCH * n_pages) // 4

    def nrm(k, shape, scale):
        return jax.random.normal(k, shape, F32) * scale

    col_scale = jnp.concatenate([
        jnp.full((w,), DEEPNORM_BETA if name in VALUE_SEGMENTS else 1.0, F32) for name, w in SEGMENTS])
    page_table = jax.random.permutation(ks[8], n_phys)[: DEC_BATCH * n_pages]
    page_table = page_table.reshape(DEC_BATCH, n_pages).astype(jnp.int32)
    return {
        'x_prompt': nrm(ks[0], (BATCH, SEQ, D_MODEL), 1.0),
        'x_sample': nrm(ks[1], (DEC_BATCH, DEC_SEQ, D_MODEL), 1.0),
        'cache_a_k': nrm(ks[2], (DEPTH, n_phys, PAGE_SIZE, A_KV_HEADS, HEAD_DIM), 1.0),
        'cache_a_v': nrm(ks[3], (DEPTH, n_phys, PAGE_SIZE, A_KV_HEADS, HEAD_DIM), 1.0),
        'cache_idx_k': nrm(ks[4], (DEPTH, n_phys, PAGE_SIZE, IDX_DIM), 1.0),
        'cache_b_k': nrm(ks[5], (DEPTH, n_phys, PAGE_SIZE, B_HEADS, 2, HEAD_DIM), 1.0),
        'cache_b_v': nrm(ks[6], (DEPTH, n_phys, PAGE_SIZE, B_HEADS, B_V_DIM), 1.0),
        'state_gla': nrm(ks[7], (DEPTH, DEC_BATCH, C_HEADS, C_K_DIM, C_V_DIM), 0.1),
        'page_table': page_table,
        'w_in': nrm(ks[9], (DEPTH, D_MODEL, W_IN_COLS), D_MODEL ** -0.5) * col_scale,
        'gla_w2': nrm(ks[10], (DEPTH, GLA_RANK, C_HEADS * C_K_DIM), GLA_RANK ** -0.5),
        'gla_b': nrm(ks[11], (DEPTH, C_HEADS * C_K_DIM), 0.1),
        'lam_q1': nrm(ks[12], (DEPTH, HEAD_DIM), 0.1),
        'lam_k1': nrm(ks[13], (DEPTH, HEAD_DIM), 0.1),
        'lam_q2': nrm(ks[14], (DEPTH, HEAD_DIM), 0.1),
        'lam_k2': nrm(ks[15], (DEPTH, HEAD_DIM), 0.1),
        'diff_subln': 1.0 + nrm(ks[16], (DEPTH, B_V_DIM), 0.02),
        'gla_norm': 1.0 + nrm(ks[17], (DEPTH, C_V_DIM), 0.02),
        'w_branch': nrm(ks[18], (DEPTH, N_BRANCH, BRANCH_WIDTH, D_MODEL), BRANCH_WIDTH ** -0.5 * DEEPNORM_BETA),
        'w_out': nrm(ks[19], (DEPTH, D_MODEL, D_MODEL), D_MODEL ** -0.5 * DEEPNORM_BETA),
        'ln1_g': 1.0 + nrm(ks[20], (DEPTH, D_MODEL), 0.02),
        'ln1_b': nrm(ks[21], (DEPTH, D_MODEL), 0.01),
        'w_up': nrm(ks[22], (DEPTH, D_MODEL, D_FF), D_MODEL ** -0.5 * DEEPNORM_BETA),
        'w_down': nrm(ks[23], (DEPTH, D_FF, D_MODEL), D_FF ** -0.5 * DEEPNORM_BETA),
        'ln2_g': 1.0 + nrm(ks[24], (DEPTH, D_MODEL), 0.02),
        'ln2_b': nrm(ks[25], (DEPTH, D_MODEL), 0.01),
    }


def reference(x_prompt, x_sample, cache_a_k, cache_a_v, cache_idx_k, cache_b_k, cache_b_v, state_gla,
              page_table, w_in, gla_w2, gla_b, lam_q1, lam_k1, lam_q2, lam_k2, diff_subln, gla_norm,
              w_branch, w_out, ln1_g, ln1_b, w_up, w_down, ln2_g, ln2_b):
    T_p = x_prompt.shape[1]
    T_s = x_sample.shape[1]
    P = page_table.shape[1] * PAGE_SIZE
    pos_p = jnp.arange(T_p)
    pos_s = P + jnp.arange(T_s)
    xp, xs = x_prompt, x_sample
    a_k_p, a_k_s, a_v_p, a_v_s, i_k_p, i_k_s = [], [], [], [], [], []
    b_k_p, b_k_s, b_v_p, b_v_s, g_p, g_s = [], [], [], [], [], []
    for l in range(DEPTH):
        lw = {
            'w_in': w_in[l], 'gla_w2': gla_w2[l], 'gla_b': gla_b[l],
            'lam_q1': lam_q1[l], 'lam_k1': lam_k1[l], 'lam_q2': lam_q2[l], 'lam_k2': lam_k2[l],
            'diff_subln': diff_subln[l], 'gla_norm': gla_norm[l], 'w_branch': w_branch[l],
            'w_out': w_out[l], 'ln1_g': ln1_g[l], 'ln1_b': ln1_b[l], 'w_up': w_up[l],
            'w_down': w_down[l], 'ln2_g': ln2_g[l], 'ln2_b': ln2_b[l],
        }
        pr = project(xp, pos_p, lw)
        oA = dsa_prompt(pr['qa'], pr['ka'], pr['va'], pr['qi'], pr['ki'], pr['wi'])
        oB = diff_attn_prompt(pr['qb'], pr['kb'], pr['vb'])
        S0 = jnp.zeros((xp.shape[0], C_HEADS, C_K_DIM, C_V_DIM), F32)
        oC, Sp = gla_chunked(pr['qc'], pr['kc'], pr['vc'], pr['log_a'], S0)
        xp = layer_tail(xp, oA, oB, oC, pr, lw, l)
        a_k_p.append(pr['ka']); a_v_p.append(pr['va']); i_k_p.append(pr['ki'])
        b_k_p.append(pr['kb']); b_v_p.append(pr['vb']); g_p.append(Sp.astype(x_prompt.dtype))
        ps = project(xs, pos_s, lw)
        oA = dsa_sample(ps['qa'], ps['ka'], ps['va'], ps['qi'], ps['ki'], ps['wi'],
                        cache_a_k, cache_a_v, cache_idx_k, page_table, l)
        oB = diff_attn_sample(ps['qb'], ps['kb'], ps['vb'], cache_b_k, cache_b_v, page_table, l)
        oC, Ss = gla_chunked(ps['qc'], ps['kc'], ps['vc'], ps['log_a'], state_gla[l])
        xs = layer_tail(xs, oA, oB, oC, ps, lw, l)
        a_k_s.append(ps['ka']); a_v_s.append(ps['va']); i_k_s.append(ps['ki'])
        b_k_s.append(ps['kb']); b_v_s.append(ps['vb']); g_s.append(Ss.astype(state_gla.dtype))
    return (xp, xs,
            jnp.stack(a_k_p), jnp.stack(a_k_s), jnp.stack(a_v_p), jnp.stack(a_v_s),
            jnp.stack(i_k_p), jnp.stack(i_k_s), jnp.stack(b_k_p), jnp.stack(b_k_s),
            jnp.stack(b_v_p), jnp.stack(b_v_s), jnp.stack(g_p), jnp.stack(g_s))
```

```python
import functools
import math

import jax
import jax.numpy as jnp
from jax import lax
from jax.experimental import pallas as pl
from jax.experimental.pallas import tpu as pltpu

F32 = jnp.float32
BF16 = jnp.bfloat16
I32 = jnp.int32

HEAD_DIM = 128
A_HEADS = 8
A_KV_HEADS = 2
A_REP = A_HEADS // A_KV_HEADS
IDX_HEADS = 16
IDX_DIM = 64
TOPK_MAX = 256
B_HEADS = 4
B_V_DIM = 256
C_HEADS = 4
C_V_DIM = 256
C_K_DIM = 128
GLA_RANK = 16
GLA_TAU = 16.0
GLA_CHUNK = 32
PAGE_SIZE = 128
Q_BLOCK = 128
ROPE_THETA = 10000.0
LN_EPS = 1e-5
RMS_EPS = 1e-6
BRANCH_WIDTH = 1024
D_MODEL = 2048

NEG = -1e30
INT_MIN = -(2 ** 31)
VMEM_LIMIT = 56 * 1024 * 1024

NT_DIMS = (((1,), (1,)), ((), ()))
TN_DIMS = (((0,), (0,)), ((), ()))

PACKED = (
    ('qa', 1024), ('qb', 1024), ('kb', 1024), ('vb', 1024), ('rc', 1024), ('vc', 1024),
    ('gate', 6144), ('qi', 1024), ('qc', 512), ('kc', 512), ('ka', 256), ('va', 256),
    ('ki', 64), ('wi', 16), ('gc', 16), ('pad', 32),
)
PACKED_OFF = {}
_o = 0
for _n, _w in PACKED:
    PACKED_OFF[_n] = (_o, _w)
    _o += _w
PACKED_COLS = _o
SRC_SEGMENTS = (
    ('qa', 1024), ('ka', 256), ('va', 256), ('qi', 1024), ('ki', 64), ('wi', 16),
    ('qb', 1024), ('kb', 1024), ('vb', 1024), ('qc', 512), ('kc', 512), ('vc', 1024),
    ('gc', 16), ('rc', 1024), ('gate', 6144),
)


def _params(sem):
    return pltpu.CompilerParams(dimension_semantics=sem, vmem_limit_bytes=VMEM_LIMIT)


def _sortable(x):
    bits = lax.bitcast_convert_type(x, I32)
    return jnp.where(bits < 0, bits ^ jnp.int32(0x7FFFFFFF), bits)


def _mm_body(x_ref, w_ref, o_ref, *scratch, nk, act):
    def finish(r):
        if act == 'relu2':
            r = jnp.square(jnp.maximum(r, 0.0))
        o_ref[...] = r.astype(o_ref.dtype)

    if nk == 1:
        finish(jnp.dot(x_ref[...], w_ref[...], preferred_element_type=F32))
        return
    acc_ref, = scratch
    k = pl.program_id(2)

    @pl.when(k == 0)
    def _():
        acc_ref[...] = jnp.zeros_like(acc_ref)

    acc_ref[...] += jnp.dot(x_ref[...], w_ref[...], preferred_element_type=F32)

    @pl.when(k == nk - 1)
    def _():
        finish(acc_ref[...])


def matmul(x, w, *, tm, tn, tk, act=None, out_dtype=F32, name='mm'):
    M, K = x.shape
    _, N = w.shape
    tm, tn, tk = min(tm, M), min(tn, N), min(tk, K)
    assert M % tm == 0 and N % tn == 0 and K % tk == 0, (M, N, K, tm, tn, tk)
    nk = K // tk
    return pl.pallas_call(
        functools.partial(_mm_body, nk=nk, act=act),
        grid=(N // tn, M // tm, nk),
        in_specs=[pl.BlockSpec((tm, tk), lambda n, m, k: (m, k)),
                  pl.BlockSpec((tk, tn), lambda n, m, k: (k, n))],
        out_specs=pl.BlockSpec((tm, tn), lambda n, m, k: (m, n)),
        out_shape=jax.ShapeDtypeStruct((M, N), out_dtype),
        scratch_shapes=[pltpu.VMEM((tm, tn), F32)] if nk > 1 else [],
        compiler_params=_params(('parallel', 'parallel', 'arbitrary')),
        name=name,
    )(x, w)


def _mm_ln_body(x_ref, w_ref, r_ref, g_ref, b_ref, o_ref, ob_ref, acc_ref, *, nk, alpha):
    k = pl.program_id(1)

    @pl.when(k == 0)
    def _():
        acc_ref[...] = jnp.zeros_like(acc_ref)

    acc_ref[...] += jnp.dot(x_ref[...], w_ref[...], preferred_element_type=F32)

    @pl.when(k == nk - 1)
    def _():
        y = alpha * r_ref[...] + acc_ref[...]
        mu = jnp.mean(y, axis=-1, keepdims=True)
        yc = y - mu
        var = jnp.mean(yc * yc, axis=-1, keepdims=True)
        out = yc * lax.rsqrt(var + LN_EPS) * g_ref[...] + b_ref[...]
        o_ref[...] = out
        ob_ref[...] = out.astype(BF16)


def matmul_ln(x, w, resid, g, b, *, alpha, tm, tk, name='mm_ln'):
    M, K = x.shape
    _, N = w.shape
    tm, tk = min(tm, M), min(tk, K)
    nk = K // tk
    return pl.pallas_call(
        functools.partial(_mm_ln_body, nk=nk, alpha=alpha),
        grid=(M // tm, nk),
        in_specs=[pl.BlockSpec((tm, tk), lambda m, k: (m, k)),
                  pl.BlockSpec((tk, N), lambda m, k: (k, 0)),
                  pl.BlockSpec((tm, N), lambda m, k: (m, 0)),
                  pl.BlockSpec((1, N), lambda m, k: (0, 0)),
                  pl.BlockSpec((1, N), lambda m, k: (0, 0))],
        out_specs=[pl.BlockSpec((tm, N), lambda m, k: (m, 0)),
                   pl.BlockSpec((tm, N), lambda m, k: (m, 0))],
        out_shape=[jax.ShapeDtypeStruct((M, N), F32), jax.ShapeDtypeStruct((M, N), BF16)],
        scratch_shapes=[pltpu.VMEM((tm, N), F32)],
        compiler_params=_params(('parallel', 'arbitrary')),
        name=name,
    )(x, w, resid, g.reshape(1, N), b.reshape(1, N))


def _branch_body(a_ref, b_ref, c_ref, w_ref, ga_ref, gb_ref, gc_ref, o_ref):
    acc = None
    for n, (br, gt) in enumerate(((a_ref, ga_ref), (b_ref, gb_ref), (c_ref, gc_ref))):
        u = jnp.dot(br[...], w_ref[n], preferred_element_type=F32)
        t = jax.nn.sigmoid(gt[...]) * u
        acc = t if acc is None else acc + t
    o_ref[...] = acc.astype(o_ref.dtype)


def branch_merge(brA, brB, brC, w_branch, z, *, tm, tn):
    M = brA.shape[0]
    tm = min(tm, M)
    g0 = PACKED_OFF['gate'][0]
    assert g0 % tn == 0 and D_MODEL % tn == 0
    gspecs = [pl.BlockSpec((tm, tn), functools.partial(lambda m, n, base: (m, base + n), base=(g0 + i * D_MODEL) // tn))
              for i in range(3)]
    return pl.pallas_call(
        _branch_body,
        grid=(M // tm, D_MODEL // tn),
        in_specs=[pl.BlockSpec((tm, BRANCH_WIDTH), lambda m, n: (m, 0))] * 3
        + [pl.BlockSpec((3, BRANCH_WIDTH, tn), lambda m, n: (0, 0, n))] + gspecs,
        out_specs=pl.BlockSpec((tm, tn), lambda m, n: (m, n)),
        out_shape=jax.ShapeDtypeStruct((M, D_MODEL), BF16),
        compiler_params=_params(('parallel', 'parallel')),
        name='branch_merge',
    )(brA, brB, brC, w_branch, z, z, z)


def _dsa_prompt_body(qi_ref, wi_ref, ki_ref, qa_ref, ka_ref, va_ref, o_ref, keys_ref, bias_ref, *, top_k, tk, T):
    QB = Q_BLOCK
    qb = pl.program_id(1)
    q0 = qb * QB
    nkt = (q0 + QB - 1) // tk + 1
    qpos = q0 + lax.broadcasted_iota(I32, (QB, tk), 0)
    lane = lax.broadcasted_iota(I32, (QB, tk), 1)
    wi = wi_ref[...]

    def score_tile(j, carry):
        off = pl.multiple_of(j * tk, tk)
        kt = ki_ref[pl.ds(off, tk), :]
        acc = jnp.zeros((QB, tk), F32)
        for h in range(IDX_HEADS):
            s = lax.dot_general(qi_ref[h], kt, NT_DIMS, preferred_element_type=F32)
            acc = acc + wi[:, h:h + 1] * jnp.maximum(s, 0.0)
        key = jnp.where(off + lane <= qpos, _sortable(acc), INT_MIN)
        keys_ref[:, pl.ds(off, tk)] = key
        return carry

    lax.fori_loop(0, nkt, score_tile, 0)

    def count(pred):
        def body(j, c):
            off = pl.multiple_of(j * tk, tk)
            hit = pred(keys_ref[:, pl.ds(off, tk)], off + lane)
            return c + jnp.sum(jnp.where(hit, 1.0, 0.0), axis=1, keepdims=True)
        return lax.fori_loop(0, nkt, body, jnp.zeros((QB, 1), F32))

    kf = float(top_k)
    c0 = count(lambda kt, kp: kt >= 0)
    lo = jnp.where(c0 >= kf, jnp.int32(0), jnp.int32(INT_MIN))

    def bit_body(i, lo):
        cand = lo + (jnp.int32(1) << (30 - i))
        c = count(lambda kt, kp: kt >= cand)
        return jnp.where(c >= kf, cand, lo)

    thr = lax.fori_loop(0, 31, bit_body, lo)

    need = kf - count(lambda kt, kp: kt > thr)
    ceq = count(lambda kt, kp: kt == thr)
    nbits = max(T.bit_length(), 1)

    def tie_fn():
        def jb(i, j):
            cand = j + (jnp.int32(1) << (nbits - 1 - i))
            g = count(lambda kt, kp: (kt == thr) & (kp < cand))
            return jnp.where(g <= need, cand, j)
        return lax.fori_loop(0, nbits, jb, jnp.zeros((QB, 1), I32))

    jstar = lax.cond(jnp.max(ceq - need) > 0.0, tie_fn, lambda: jnp.full((QB, 1), 2 ** 30, I32))

    def bias_tile(j, carry):
        off = pl.multiple_of(j * tk, tk)
        kt = keys_ref[:, pl.ds(off, tk)]
        kp = off + lane
        sel = ((kt > thr) | ((kt == thr) & (kp < jstar))) & (kp <= qpos)
        bias_ref[:, pl.ds(off, tk)] = jnp.where(sel, 0.0, NEG)
        return carry

    lax.fori_loop(0, nkt, bias_tile, 0)

    for g in range(A_KV_HEADS):
        qg = jnp.concatenate([qa_ref[A_REP * g + r] for r in range(A_REP)], axis=0)

        def att_tile(j, carry, g=g, qg=qg):
            m, l, acc = carry
            off = pl.multiple_of(j * tk, tk)
            kt = ka_ref[g, pl.ds(off, tk), :]
            vt = va_ref[g, pl.ds(off, tk), :]
            s = lax.dot_general(qg, kt, NT_DIMS, preferred_element_type=F32)
            b = bias_ref[:, pl.ds(off, tk)]
            s = (s.reshape(A_REP, QB, tk) + b[None]).reshape(A_REP * QB, tk)
            m_new = jnp.maximum(m, jnp.max(s, axis=1, keepdims=True))
            corr = jnp.exp(m - m_new)
            p = jnp.exp(s - m_new)
            l = l * corr + jnp.sum(p, axis=1, keepdims=True)
            acc = acc * corr + jnp.dot(p.astype(BF16), vt, preferred_element_type=F32)
            return m_new, l, acc

        init = (jnp.full((A_REP * QB, 1), NEG, F32), jnp.zeros((A_REP * QB, 1), F32),
                jnp.zeros((A_REP * QB, HEAD_DIM), F32))
        m, l, acc = lax.fori_loop(0, nkt, att_tile, init)
        out = acc / l
        for r in range(A_REP):
            h = A_REP * g + r
            o_ref[:, h * HEAD_DIM:(h + 1) * HEAD_DIM] = out[r * QB:(r + 1) * QB].astype(o_ref.dtype)


def dsa_prompt(qi, wi, ki, qa, ka, va):
    B, _, T, _ = qi.shape
    top_k = min(TOPK_MAX, T // 4)
    tk = min(512, T)
    nb = T // Q_BLOCK
    return pl.pallas_call(
        functools.partial(_dsa_prompt_body, top_k=top_k, tk=tk, T=T),
        grid=(B, nb),
        in_specs=[pl.BlockSpec((None, IDX_HEADS, Q_BLOCK, IDX_DIM), lambda b, q: (b, 0, q, 0)),
                  pl.BlockSpec((None, Q_BLOCK, IDX_HEADS), lambda b, q: (b, q, 0)),
                  pl.BlockSpec((None, T, IDX_DIM), lambda b, q: (b, 0, 0)),
                  pl.BlockSpec((None, A_HEADS, Q_BLOCK, HEAD_DIM), lambda b, q: (b, 0, q, 0)),
                  pl.BlockSpec((None, A_KV_HEADS, T, HEAD_DIM), lambda b, q: (b, 0, 0, 0)),
                  pl.BlockSpec((None, A_KV_HEADS, T, HEAD_DIM), lambda b, q: (b, 0, 0, 0))],
        out_specs=pl.BlockSpec((None, Q_BLOCK, A_HEADS * HEAD_DIM), lambda b, q: (b, q, 0)),
        out_shape=jax.ShapeDtypeStruct((B, T, A_HEADS * HEAD_DIM), BF16),
        scratch_shapes=[pltpu.VMEM((Q_BLOCK, T), I32), pltpu.VMEM((Q_BLOCK, T), F32)],
        compiler_params=_params(('parallel', 'arbitrary')),
        name='dsa_prompt',
    )(qi, wi, ki, qa, ka, va)


def _lam(lq1, lk1, lq2, lk2, lam_init):
    return (jnp.exp(jnp.sum(lq1[...] * lk1[...], axis=-1, keepdims=True))
            - jnp.exp(jnp.sum(lq2[...] * lk2[...], axis=-1, keepdims=True)) + lam_init)


def _diff_finish(o0, o1, lam, subln, lam_init):
    a = o0 - lam * o1
    a = a * lax.rsqrt(jnp.mean(a * a, axis=-1, keepdims=True) + RMS_EPS) * subln
    return a * (1.0 - lam_init)


def _diff_prompt_body(lq1, lk1, lq2, lk2, sub_ref, q_ref, k_ref, v_ref, o_ref, *, tq, tk, lam_init):
    qi = pl.program_id(2)
    q0 = qi * tq
    nkt = (q0 + tq - 1) // tk + 1
    qpos = q0 + lax.broadcasted_iota(I32, (tq, tk), 0)
    lane = lax.broadcasted_iota(I32, (tq, tk), 1)
    outs = []
    for mp in range(2):
        q = q_ref[mp]

        def tile(j, carry, mp=mp, q=q):
            m, l, acc = carry
            off = pl.multiple_of(j * tk, tk)
            kt = k_ref[mp, pl.ds(off, tk), :]
            vt = v_ref[pl.ds(off, tk), :]
            s = lax.dot_general(q, kt, NT_DIMS, preferred_element_type=F32)
            s = jnp.where(off + lane <= qpos, s, NEG)
            m_new = jnp.maximum(m, jnp.max(s, axis=1, keepdims=True))
            corr = jnp.exp(m - m_new)
            p = jnp.exp(s - m_new)
            l = l * corr + jnp.sum(p, axis=1, keepdims=True)
            acc = acc * corr + jnp.dot(p.astype(BF16), vt, preferred_element_type=F32)
            return m_new, l, acc

        init = (jnp.full((tq, 1), NEG, F32), jnp.zeros((tq, 1), F32), jnp.zeros((tq, B_V_DIM), F32))
        m, l, acc = lax.fori_loop(0, nkt, tile, init)
        outs.append(acc / l)
    lam = _lam(lq1, lk1, lq2, lk2, lam_init)
    o_ref[...] = _diff_finish(outs[0], outs[1], lam, sub_ref[...], lam_init).astype(o_ref.dtype)


def diff_prompt(qb, kb, vb, lam_q1, lam_k1, lam_q2, lam_k2, subln, lam_init):
    B, H, _, T, _ = qb.shape
    tq = min(512, T)
    tk = min(512, T)
    vec = pl.BlockSpec((1, HEAD_DIM), lambda b, h, q: (0, 0))
    return pl.pallas_call(
        functools.partial(_diff_prompt_body, tq=tq, tk=tk, lam_init=lam_init),
        grid=(B, H, T // tq),
        in_specs=[vec, vec, vec, vec,
                  pl.BlockSpec((1, B_V_DIM), lambda b, h, q: (0, 0)),
                  pl.BlockSpec((None, None, 2, tq, HEAD_DIM), lambda b, h, q: (b, h, 0, q, 0)),
                  pl.BlockSpec((None, None, 2, T, HEAD_DIM), lambda b, h, q: (b, h, 0, 0, 0)),
                  pl.BlockSpec((None, None, T, B_V_DIM), lambda b, h, q: (b, h, 0, 0))],
        out_specs=pl.BlockSpec((None, tq, B_V_DIM), lambda b, h, q: (b, q, h)),
        out_shape=jax.ShapeDtypeStruct((B, T, H * B_V_DIM), BF16),
        compiler_params=_params(('parallel', 'parallel', 'arbitrary')),
        name='diff_prompt',
    )(lam_q1.reshape(1, -1), lam_k1.reshape(1, -1), lam_q2.reshape(1, -1), lam_k2.reshape(1, -1),
      subln.reshape(1, -1), qb, kb, vb)


def _gla_body(q_ref, k_ref, v_ref, gc_ref, rc_ref, w2_ref, gb_ref, gn_ref, s0_ref, o_ref, sfin_ref, st_ref,
              *, tb, chunk, t_valid, nt):
    t = pl.program_id(1)

    @pl.when(t == 0)
    def _():
        st_ref[...] = s0_ref[...]

    x = jnp.dot(gc_ref[...].astype(BF16), w2_ref[...], preferred_element_type=F32) + gb_ref[...]
    la = (jnp.minimum(x, 0.0) - jnp.log(1.0 + jnp.exp(-jnp.abs(x)))) * (1.0 / GLA_TAU)
    W = C_HEADS * C_K_DIM
    row = lax.broadcasted_iota(I32, (tb, W), 0)
    if t_valid is not None:
        la = jnp.where(t * tb + row < t_valid, la, 0.0)
    rowc = row % chunk
    b = la
    sh = 1
    while sh < chunk:
        b = b + jnp.where(rowc >= sh, pltpu.roll(b, sh, axis=0), 0.0)
        sh *= 2
    k = k_ref[...]
    qe = (q_ref[...] * (C_K_DIM ** -0.5) * jnp.exp(b)).astype(BF16)
    ke = (k * jnp.exp(-b)).astype(BF16)
    v16 = v_ref[...].astype(BF16)
    rc = rc_ref[...]
    gn = gn_ref[...]
    tril = lax.broadcasted_iota(I32, (chunk, chunk), 0) >= lax.broadcasted_iota(I32, (chunk, chunk), 1)
    for c in range(tb // chunk):
        r0 = c * chunk
        bl = b[r0 + chunk - 1:r0 + chunk, :]
        kd = (k[r0:r0 + chunk] * jnp.exp(bl - b[r0:r0 + chunk])).astype(BF16)
        dec = jnp.exp(bl)
        for h in range(C_HEADS):
            ck = slice(h * C_K_DIM, (h + 1) * C_K_DIM)
            cv = slice(h * C_V_DIM, (h + 1) * C_V_DIM)
            qe_c = qe[r0:r0 + chunk, ck]
            v_c = v16[r0:r0 + chunk, cv]
            att = lax.dot_general(qe_c, ke[r0:r0 + chunk, ck], NT_DIMS, preferred_element_type=F32)
            att = jnp.where(tril, att, 0.0).astype(BF16)
            st = st_ref[h]
            o_c = (lax.dot_general(qe_c, st.astype(BF16), NT_DIMS, preferred_element_type=F32)
                   + jnp.dot(att, v_c, preferred_element_type=F32))
            st_ref[h] = st * dec[:, ck] + lax.dot_general(v_c, kd[:, ck], TN_DIMS, preferred_element_type=F32)
            o_n = o_c * lax.rsqrt(jnp.mean(o_c * o_c, axis=-1, keepdims=True) + RMS_EPS) * gn
            r_c = rc[r0:r0 + chunk, cv]
            o_ref[r0:r0 + chunk, cv] = (o_n * (r_c * jax.nn.sigmoid(r_c))).astype(o_ref.dtype)

    @pl.when(t == nt - 1)
    def _():
        sfin_ref[...] = st_ref[...]


def gla(q, k, v, gc, rc, w2, gb, gn, s0t, *, t_valid=None):
    B, T, _ = q.shape
    chunk = GLA_CHUNK
    tb = min(256, T)
    nt = T // tb
    W = C_HEADS * C_K_DIM
    V = C_HEADS * C_V_DIM
    tok = lambda w: pl.BlockSpec((None, tb, w), lambda b, t: (b, t, 0))
    st_spec = pl.BlockSpec((None, C_HEADS, C_V_DIM, C_K_DIM), lambda b, t: (b, 0, 0, 0))
    return pl.pallas_call(
        functools.partial(_gla_body, tb=tb, chunk=chunk, t_valid=t_valid, nt=nt),
        grid=(B, nt),
        in_specs=[tok(W), tok(W), tok(V), tok(GLA_RANK), tok(V),
                  pl.BlockSpec((GLA_RANK, W), lambda b, t: (0, 0)),
                  pl.BlockSpec((1, W), lambda b, t: (0, 0)),
                  pl.BlockSpec((1, C_V_DIM), lambda b, t: (0, 0)),
                  st_spec],
        out_specs=[tok(V), st_spec],
        out_shape=[jax.ShapeDtypeStruct((B, T, V), BF16),
                   jax.ShapeDtypeStruct((B, C_HEADS, C_V_DIM, C_K_DIM), F32)],
        scratch_shapes=[pltpu.VMEM((C_HEADS, C_V_DIM, C_K_DIM), F32)],
        compiler_params=_params(('parallel', 'arbitrary')),
        name='gla',
    )(q, k, v, gc, rc, w2.astype(BF16), gb.reshape(1, W), gn.reshape(1, C_V_DIM), s0t)


def _page_specs(block, layer, pps, n_extra_axes=0):
    def mk(i):
        return pl.BlockSpec((None, None) + block,
                            lambda b, p, pt: (layer, pt[b, p * pps + i]) + (0,) * len(block))
    return [mk(i) for i in range(pps)]


def _dsa_sample_score_body(pt_ref, qi_ref, wi_ref, kin_ref, *rest, pps, n_pages):
    pages, o_ref = rest[:pps], rest[pps]
    p = pl.program_id(1)
    qi = qi_ref[...]
    wi = wi_ref[...]

    @pl.when(p == 0)
    def _():
        o_ref[...] = jnp.full(o_ref.shape, -jnp.inf, F32)
        s = jnp.sum(qi.astype(F32) * kin_ref[...], axis=1, keepdims=True)
        snew = jnp.sum(wi * jnp.maximum(s, 0.0), axis=0, keepdims=True)
        lane = lax.broadcasted_iota(I32, (1, PAGE_SIZE), 1)
        o_ref[n_pages:n_pages + 1, :] = jnp.where(lane == 0, snew, -jnp.inf)

    for i in range(pps):
        kp = pages[i][...].astype(BF16)
        s = lax.dot_general(qi, kp, NT_DIMS, preferred_element_type=F32)
        o_ref[pl.ds(p * pps + i, 1), :] = jnp.sum(wi * jnp.maximum(s, 0.0), axis=0, keepdims=True)


def _topk_bias(sc, top_k, n_valid):
    R = sc.shape[0]
    key = _sortable(sc)
    pos = lax.broadcasted_iota(I32, sc.shape, 0) * PAGE_SIZE + lax.broadcasted_iota(I32, sc.shape, 1)
    key = jnp.where(pos < n_valid, key, INT_MIN)

    def count(hit):
        c = jnp.sum(jnp.where(hit, 1.0, 0.0), axis=1, keepdims=True)
        return jnp.sum(c, axis=0, keepdims=True)

    kf = float(top_k)
    lo = jnp.where(count(key >= 0) >= kf, jnp.int32(0), jnp.int32(INT_MIN))

    def bit_body(i, lo):
        cand = lo + (jnp.int32(1) << (30 - i))
        return jnp.where(count(key >= cand) >= kf, cand, lo)

    thr = lax.fori_loop(0, 31, bit_body, lo)
    need = kf - count(key > thr)
    nbits = (R * PAGE_SIZE).bit_length()

    def jb(i, j):
        cand = j + (jnp.int32(1) << (nbits - 1 - i))
        return jnp.where(count((key == thr) & (pos < cand)) <= need, cand, j)

    jstar = lax.fori_loop(0, nbits, jb, jnp.zeros((1, 1), I32))
    sel = ((key > thr) | ((key == thr) & (pos < jstar))) & (pos < n_valid)
    return jnp.where(sel, 0.0, NEG)


def _dsa_sample_attn_body(pt_ref, sc_ref, q_ref, kn_ref, vn_ref, *rest, pps, n_pages, top_k):
    kpages, vpages = rest[:pps], rest[pps:2 * pps]
    o_ref, bias_ref, m_ref, l_ref, acc_ref = rest[2 * pps:]
    p = pl.program_id(1)
    q = q_ref[...]
    first = lax.broadcasted_iota(I32, (A_HEADS, PAGE_SIZE), 0) < A_REP

    @pl.when(p == 0)
    def _():
        bias_ref[...] = _topk_bias(sc_ref[...], top_k, n_pages * PAGE_SIZE + 1)
        m_ref[...] = jnp.full(m_ref.shape, NEG, F32)
        l_ref[...] = jnp.zeros_like(l_ref)
        acc_ref[...] = jnp.zeros_like(acc_ref)

    for i in range(pps):
        kp = kpages[i][...].astype(BF16)
        vp = vpages[i][...].astype(BF16)
        s0 = lax.dot_general(q, kp[:, :HEAD_DIM], NT_DIMS, preferred_element_type=F32)
        s1 = lax.dot_general(q, kp[:, HEAD_DIM:], NT_DIMS, preferred_element_type=F32)
        s = jnp.where(first, s0, s1) + bias_ref[pl.ds(p * pps + i, 1), :]
        m = m_ref[...]
        m_new = jnp.maximum(m, jnp.max(s, axis=1, keepdims=True))
        corr = jnp.exp(m - m_new)
        pr = jnp.exp(s - m_new)
        l_ref[...] = l_ref[...] * corr + jnp.sum(pr, axis=1, keepdims=True)
        pb = pr.astype(BF16)
        pv = jnp.where(first, jnp.dot(pb, vp[:, :HEAD_DIM], preferred_element_type=F32),
                       jnp.dot(pb, vp[:, HEAD_DIM:], preferred_element_type=F32))
        acc_ref[...] = acc_ref[...] * corr + pv
        m_ref[...] = m_new

    @pl.when(p == pl.num_programs(1) - 1)
    def _():
        s = (jnp.sum(q.astype(F32) * kn_ref[...], axis=1, keepdims=True)
             + bias_ref[n_pages:n_pages + 1, 0:1])
        m = m_ref[...]
        m_new = jnp.maximum(m, s)
        corr = jnp.exp(m - m_new)
        pr = jnp.exp(s - m_new)
        l = l_ref[...] * corr + pr
        o_ref[...] = (acc_ref[...] * corr + pr * vn_ref[...]) / l


def dsa_sample(layer, page_table, qi, wi, ki_new, qa, ka_new, va_new, cache_idx_k, cache_a_k, cache_a_v):
    DB, n_pages = page_table.shape
    pps = math.gcd(8, n_pages)
    R = ((n_pages + 1 + 7) // 8) * 8
    L = n_pages * PAGE_SIZE + 1
    top_k = min(TOPK_MAX, L // 4)
    per_b = lambda *blk: pl.BlockSpec((None,) + blk, lambda b, p, pt: (b,) + (0,) * len(blk))
    scores = pl.pallas_call(
        functools.partial(_dsa_sample_score_body, pps=pps, n_pages=n_pages),
        grid_spec=pltpu.PrefetchScalarGridSpec(
            num_scalar_prefetch=1, grid=(DB, n_pages // pps),
            in_specs=[per_b(IDX_HEADS, IDX_DIM), per_b(IDX_HEADS, 1), per_b(1, IDX_DIM)]
            + _page_specs((PAGE_SIZE, IDX_DIM), layer, pps),
            out_specs=per_b(R, PAGE_SIZE)),
        out_shape=jax.ShapeDtypeStruct((DB, R, PAGE_SIZE), F32),
        compiler_params=_params(('parallel', 'arbitrary')),
        name='dsa_sample_scores',
    )(page_table, qi, wi, ki_new, *([cache_idx_k] * pps))
    return pl.pallas_call(
        functools.partial(_dsa_sample_attn_body, pps=pps, n_pages=n_pages, top_k=top_k),
        grid_spec=pltpu.PrefetchScalarGridSpec(
            num_scalar_prefetch=1, grid=(DB, n_pages // pps),
            in_specs=[per_b(R, PAGE_SIZE), per_b(A_HEADS, HEAD_DIM), per_b(A_HEADS, HEAD_DIM), per_b(A_HEADS, HEAD_DIM)]
            + _page_specs((PAGE_SIZE, A_KV_HEADS * HEAD_DIM), layer, pps) * 1
            + _page_specs((PAGE_SIZE, A_KV_HEADS * HEAD_DIM), layer, pps),
            out_specs=per_b(A_HEADS, HEAD_DIM),
            scratch_shapes=[pltpu.VMEM((R, PAGE_SIZE), F32), pltpu.VMEM((A_HEADS, 1), F32),
                            pltpu.VMEM((A_HEADS, 1), F32), pltpu.VMEM((A_HEADS, HEAD_DIM), F32)]),
        out_shape=jax.ShapeDtypeStruct((DB, A_HEADS, HEAD_DIM), F32),
        compiler_params=_params(('parallel', 'arbitrary')),
        name='dsa_sample_attn',
    )(page_table, scores, qa, ka_new, va_new, *([cache_a_k] * pps), *([cache_a_v] * pps))


def _diff_sample_body(pt_ref, lq1, lk1, lq2, lk2, sub_ref, q_ref, kn_ref, vn_ref, *rest, pps, lam_init):
    kpages, vpages = rest[:pps], rest[pps:2 * pps]
    o_ref, m_ref, l_ref, acc_ref = rest[2 * pps:]
    p = pl.program_id(1)
    NJ = 2 * B_HEADS
    q = q_ref[...]
    rowk = lax.broadcasted_iota(I32, (NJ, PAGE_SIZE), 0)
    rowv = lax.broadcasted_iota(I32, (NJ, B_V_DIM), 0)

    @pl.when(p == 0)
    def _():
        m_ref[...] = jnp.full(m_ref.shape, NEG, F32)
        l_ref[...] = jnp.zeros_like(l_ref)
        acc_ref[...] = jnp.zeros_like(acc_ref)

    for i in range(pps):
        kp = kpages[i][...].astype(BF16)
        vp = vpages[i][...].astype(BF16)
        s = jnp.zeros((NJ, PAGE_SIZE), F32)
        for j in range(NJ):
            sj = lax.dot_general(q, kp[:, j * HEAD_DIM:(j + 1) * HEAD_DIM], NT_DIMS, preferred_element_type=F32)
            s = jnp.where(rowk == j, sj, s)
        m = m_ref[...]
        m_new = jnp.maximum(m, jnp.max(s, axis=1, keepdims=True))
        corr = jnp.exp(m - m_new)
        pr = jnp.exp(s - m_new)
        l_ref[...] = l_ref[...] * corr + jnp.sum(pr, axis=1, keepdims=True)
        pb = pr.astype(BF16)
        pv = jnp.zeros((NJ, B_V_DIM), F32)
        for h in range(B_HEADS):
            ph = jnp.dot(pb, vp[:, h * B_V_DIM:(h + 1) * B_V_DIM], preferred_element_type=F32)
            pv = jnp.where(rowv // 2 == h, ph, pv)
        acc_ref[...] = acc_ref[...] * corr + pv
        m_ref[...] = m_new

    @pl.when(p == pl.num_programs(1) - 1)
    def _():
        s = jnp.sum(q.astype(F32) * kn_ref[...], axis=1, keepdims=True)
        m = m_ref[...]
        m_new = jnp.maximum(m, s)
        corr = jnp.exp(m - m_new)
        pr = jnp.exp(s - m_new)
        l = l_ref[...] * corr + pr
        o = (acc_ref[...] * corr + pr * vn_ref[...]) / l
        lam = _lam(lq1, lk1, lq2, lk2, lam_init)
        for h in range(B_HEADS):
            o_ref[h:h + 1, :] = _diff_finish(o[2 * h:2 * h + 1], o[2 * h + 1:2 * h + 2], lam, sub_ref[...], lam_init)


def diff_sample(layer, page_table, qb, kb_new, vb_new, cache_b_k, cache_b_v,
                lam_q1, lam_k1, lam_q2, lam_k2, subln, lam_init):
    DB, n_pages = page_table.shape
    pps = math.gcd(8, n_pages)
    NJ = 2 * B_HEADS
    per_b = lambda *blk: pl.BlockSpec((None,) + blk, lambda b, p, pt: (b,) + (0,) * len(blk))
    vec = lambda w: pl.BlockSpec((1, w), lambda b, p, pt: (0, 0))
    return pl.pallas_call(
        functools.partial(_diff_sample_body, pps=pps, lam_init=lam_init),
        grid_spec=pltpu.PrefetchScalarGridSpec(
            num_scalar_prefetch=1, grid=(DB, n_pages // pps),
            in_specs=[vec(HEAD_DIM)] * 4 + [vec(B_V_DIM),
                      per_b(NJ, HEAD_DIM), per_b(NJ, HEAD_DIM), per_b(NJ, B_V_DIM)]
            + _page_specs((PAGE_SIZE, NJ * HEAD_DIM), layer, pps)
            + _page_specs((PAGE_SIZE, B_HEADS * B_V_DIM), layer, pps),
            out_specs=per_b(B_HEADS, B_V_DIM),
            scratch_shapes=[pltpu.VMEM((NJ, 1), F32), pltpu.VMEM((NJ, 1), F32), pltpu.VMEM((NJ, B_V_DIM), F32)]),
        out_shape=jax.ShapeDtypeStruct((DB, B_HEADS, B_V_DIM), F32),
        compiler_params=_params(('parallel', 'arbitrary')),
        name='diff_sample',
    )(page_table, lam_q1.reshape(1, -1), lam_k1.reshape(1, -1), lam_q2.reshape(1, -1), lam_k2.reshape(1, -1),
      subln.reshape(1, -1), qb, kb_new, vb_new, *([cache_b_k] * pps), *([cache_b_v] * pps))


def _pack_w_in(w):
    src = {}
    off = 0
    for name, width in SRC_SEGMENTS:
        src[name] = w[:, off:off + width]
        off += width
    cols = [jnp.zeros((w.shape[0], width), w.dtype) if name == 'pad' else src[name] for name, width in PACKED]
    return jnp.concatenate(cols, axis=1).astype(BF16)


def _seg(z, name):
    o, w = PACKED_OFF[name]
    return z[..., o:o + w]


def _rope(x, cos, sin):
    half = x.shape[-1] // 2
    c = cos[:, None, :]
    s = sin[:, None, :]
    x1, x2 = x[..., :half], x[..., half:]
    return jnp.concatenate([x1 * c - x2 * s, x2 * c + x1 * s], axis=-1)


def _rope_tables(pos, dh):
    half = dh // 2
    inv_freq = ROPE_THETA ** (-jnp.arange(half, dtype=F32) / half)
    ang = pos.astype(F32)[:, None] * inv_freq[None, :]
    return jnp.cos(ang), jnp.sin(ang)


def _project(x16, w_packed, Bn, T, pos):
    z = matmul(x16, w_packed, tm=1024, tn=1152, tk=2048, name='in_proj')
    zz = z.reshape(Bn, T, PACKED_COLS)
    c128, s128 = _rope_tables(pos, HEAD_DIM)
    c64, s64 = _rope_tables(pos, IDX_DIM)
    pr = {
        'qa': _rope(_seg(zz, 'qa').reshape(Bn, T, A_HEADS, HEAD_DIM), c128, s128),
        'ka': _rope(_seg(zz, 'ka').reshape(Bn, T, A_KV_HEADS, HEAD_DIM), c128, s128),
        'va': _seg(zz, 'va').reshape(Bn, T, A_KV_HEADS, HEAD_DIM),
        'qi': _rope(_seg(zz, 'qi').reshape(Bn, T, IDX_HEADS, IDX_DIM), c64, s64) * IDX_DIM ** -0.5,
        'ki': _rope(_seg(zz, 'ki').reshape(Bn, T, 1, IDX_DIM), c64, s64).reshape(Bn, T, IDX_DIM),
        'wi': _seg(zz, 'wi') * IDX_HEADS ** -0.5,
        'qb': _rope(_seg(zz, 'qb').reshape(Bn, T, 2 * B_HEADS, HEAD_DIM), c128, s128),
        'kb': _rope(_seg(zz, 'kb').reshape(Bn, T, 2 * B_HEADS, HEAD_DIM), c128, s128),
        'vb': _seg(zz, 'vb').reshape(Bn, T, B_HEADS, B_V_DIM),
    }
    return z, zz, pr


def _tail(x, x16, z, oA, oB, oC, lw, alpha):
    g = branch_merge(oA, oB, oC, lw['w_branch'], z, tm=512, tn=1024)
    x1, x1b = matmul_ln(g, lw['w_out'], x, lw['ln1_g'], lw['ln1_b'], alpha=alpha, tm=512, tk=1024, name='out_ln1')
    hid = matmul(x1b, lw['w_up'], tm=1024, tn=1024, tk=2048, act='relu2', out_dtype=BF16, name='mlp_up')
    return matmul_ln(hid, lw['w_down'], x1, lw['ln2_g'], lw['ln2_b'], alpha=alpha, tm=512, tk=1024, name='down_ln2')


def kernel(x_prompt, x_sample, cache_a_k, cache_a_v, cache_idx_k, cache_b_k, cache_b_v, state_gla, page_table,
           w_in, gla_w2, gla_b, lam_q1, lam_k1, lam_q2, lam_k2, diff_subln, gla_norm, w_branch, w_out,
           ln1_g, ln1_b, w_up, w_down, ln2_g, ln2_b):
    B, T, D = x_prompt.shape
    DB, Ts, _ = x_sample.shape
    assert Ts == 1
    DEPTH = w_in.shape[0]
    n_pages = page_table.shape[1]
    P = n_pages * PAGE_SIZE
    alpha = (2.0 * DEPTH) ** 0.25
    n_phys = cache_a_k.shape[1]
    ca_k = cache_a_k.reshape(DEPTH, n_phys, PAGE_SIZE, A_KV_HEADS * HEAD_DIM)
    ca_v = cache_a_v.reshape(DEPTH, n_phys, PAGE_SIZE, A_KV_HEADS * HEAD_DIM)
    cb_k = cache_b_k.reshape(DEPTH, n_phys, PAGE_SIZE, B_HEADS * 2 * HEAD_DIM)
    cb_v = cache_b_v.reshape(DEPTH, n_phys, PAGE_SIZE, B_HEADS * B_V_DIM)
    pos_p = jnp.arange(T)
    pos_s = P + jnp.arange(Ts)
    TS_PAD = GLA_CHUNK

    xp = x_prompt.reshape(B * T, D)
    xs = x_sample.reshape(DB * Ts, D)
    xp16, xs16 = xp.astype(BF16), xs.astype(BF16)
    outs = {k: [] for k in ('a_k_p', 'a_k_s', 'a_v_p', 'a_v_s', 'i_k_p', 'i_k_s', 'b_k_p', 'b_k_s',
                            'b_v_p', 'b_v_s', 'g_p', 'g_s')}
    for l in range(DEPTH):
        lam_init = 0.8 - 0.6 * math.exp(-0.3 * l)
        lw = {'w_branch': w_branch[l].astype(BF16), 'w_out': w_out[l].astype(BF16), 'w_up': w_up[l].astype(BF16),
              'w_down': w_down[l].astype(BF16), 'ln1_g': ln1_g[l], 'ln1_b': ln1_b[l], 'ln2_g': ln2_g[l],
              'ln2_b': ln2_b[l]}
        w_packed = _pack_w_in(w_in[l])

        z, zz, pr = _project(xp16, w_packed, B, T, pos_p)
        oA = dsa_prompt(
            jnp.transpose(pr['qi'], (0, 2, 1, 3)).astype(BF16), pr['wi'], pr['ki'].astype(BF16),
            jnp.transpose(pr['qa'] * HEAD_DIM ** -0.5, (0, 2, 1, 3)).astype(BF16),
            jnp.transpose(pr['ka'], (0, 2, 1, 3)).astype(BF16), jnp.transpose(pr['va'], (0, 2, 1, 3)).astype(BF16))
        to_hm = lambda a: jnp.transpose(a.reshape(B, T, B_HEADS, 2, HEAD_DIM), (0, 2, 3, 1, 4)).astype(BF16)
        oB = diff_prompt(to_hm(pr['qb'] * HEAD_DIM ** -0.5), to_hm(pr['kb']),
                         jnp.transpose(pr['vb'], (0, 2, 1, 3)).astype(BF16),
                         lam_q1[l], lam_k1[l], lam_q2[l], lam_k2[l], diff_subln[l], lam_init)
        oC, Sp = gla(_seg(zz, 'qc'), _seg(zz, 'kc'), _seg(zz, 'vc'), _seg(zz, 'gc'), _seg(zz, 'rc'),
                     gla_w2[l], gla_b[l], gla_norm[l], jnp.zeros((B, C_HEADS, C_V_DIM, C_K_DIM), F32))
        xp, xp16 = _tail(xp, xp16, z, oA.reshape(B * T, -1), oB.reshape(B * T, -1), oC.reshape(B * T, -1), lw, alpha)
        outs['a_k_p'].append(pr['ka']); outs['a_v_p'].append(pr['va']); outs['i_k_p'].append(pr['ki'])
        outs['b_k_p'].append(pr['kb'].reshape(B, T, B_HEADS, 2, HEAD_DIM)); outs['b_v_p'].append(pr['vb'])
        outs['g_p'].append(jnp.swapaxes(Sp, -1, -2))

        z, zz, ps = _project(xs16, w_packed, DB, Ts, pos_s)
        rep = lambda a: jnp.repeat(a[:, 0], A_REP, axis=1)
        oA = dsa_sample(l, page_table,
                        ps['qi'][:, 0].astype(BF16), ps['wi'].reshape(DB, IDX_HEADS, 1), ps['ki'],
                        (ps['qa'][:, 0] * HEAD_DIM ** -0.5).astype(BF16), rep(ps['ka']), rep(ps['va']),
                        cache_idx_k, ca_k, ca_v)
        oB = diff_sample(l, page_table, (ps['qb'][:, 0] * HEAD_DIM ** -0.5).astype(BF16), ps['kb'][:, 0],
                         jnp.repeat(ps['vb'][:, 0], 2, axis=1), cb_k, cb_v,
                         lam_q1[l], lam_k1[l], lam_q2[l], lam_k2[l], diff_subln[l], lam_init)
        padt = lambda a: jnp.pad(a, ((0, 0), (0, TS_PAD - Ts), (0, 0)))
        oC, Ss = gla(padt(_seg(zz, 'qc')), padt(_seg(zz, 'kc')), padt(_seg(zz, 'vc')), padt(_seg(zz, 'gc')),
                     padt(_seg(zz, 'rc')), gla_w2[l], gla_b[l], gla_norm[l],
                     jnp.swapaxes(state_gla[l], -1, -2), t_valid=Ts)
        xs, xs16 = _tail(xs, xs16, z, oA.reshape(DB, -1).astype(BF16), oB.reshape(DB, -1).astype(BF16),
                         oC[:, :Ts].reshape(DB * Ts, -1), lw, alpha)
        outs['a_k_s'].append(ps['ka']); outs['a_v_s'].append(ps['va']); outs['i_k_s'].append(ps['ki'])
        outs['b_k_s'].append(ps['kb'].reshape(DB, Ts, B_HEADS, 2, HEAD_DIM)); outs['b_v_s'].append(ps['vb'])
        outs['g_s'].append(jnp.swapaxes(Ss, -1, -2))

    st = {k: jnp.stack(v) for k, v in outs.items()}
    return (xp.reshape(B, T, D), xs.reshape(DB, Ts, D),
            st['a_k_p'], st['a_k_s'], st['a_v_p'], st['a_v_s'], st['i_k_p'], st['i_k_s'],
            st['b_k_p'], st['b_k_s'], st['b_v_p'], st['b_v_s'], st['g_p'], st['g_s'])
```

```python
import functools
import math

import jax
import jax.numpy as jnp
from jax import lax
from jax.experimental import pallas as pl
from jax.experimental.pallas import tpu as pltpu

F32 = jnp.float32
BF16 = jnp.bfloat16
I32 = jnp.int32

LANES = 128
HEAD_DIM = 128
A_HEADS = 8
A_KV_HEADS = 2
A_REP = A_HEADS // A_KV_HEADS
IDX_HEADS = 16
IDX_DIM = 64
TOPK_MAX = 256
B_HEADS = 4
B_V_DIM = 256
C_HEADS = 4
C_V_DIM = 256
C_K_DIM = 128
GLA_RANK = 16
GLA_TAU = 16.0
GLA_CHUNK = 32
PAGE_SIZE = 128
Q_BLOCK = 128
ROPE_THETA = 10000.0
LN_EPS = 1e-5
RMS_EPS = 1e-6
BRANCH_WIDTH = 1024
D_MODEL = 2048

NEG = -1e30
INT_MIN = -(2 ** 31)
VMEM_LIMIT = 56 * 1024 * 1024

NT_DIMS = (((1,), (1,)), ((), ()))
TN_DIMS = (((0,), (0,)), ((), ()))

MISC_WI = IDX_DIM
MISC_GC = IDX_DIM + IDX_HEADS

SRC_SEGMENTS = (
    ('qa', 1024), ('ka', 256), ('va', 256), ('qi', 1024), ('ki', 64), ('wi', 16),
    ('qb', 1024), ('kb', 1024), ('vb', 1024), ('qc', 512), ('kc', 512), ('vc', 1024),
    ('gc', 16), ('rc', 1024), ('gate', 6144),
)


def _params(sem):
    return pltpu.CompilerParams(dimension_semantics=sem, vmem_limit_bytes=VMEM_LIMIT)


def _sortable(x):
    bits = lax.bitcast_convert_type(x, I32)
    return jnp.where(bits < 0, bits ^ jnp.int32(0x7FFFFFFF), bits)


def _swap_halves(blk, dh):
    if dh == LANES:
        return pltpu.roll(blk, LANES // 2, axis=1)
    lane = lax.broadcasted_iota(I32, blk.shape, 1)
    half = dh // 2
    return jnp.where(lane % dh < half, pltpu.roll(blk, LANES - half, axis=1), pltpu.roll(blk, half, axis=1))


def _proj_body(x_ref, w_ref, *rest, rope, scale, emit_kz):
    if rope:
        cos_ref, sin_ref = rest[:2]
        outs = rest[2:]
    else:
        outs = rest
    r = jnp.dot(x_ref[...], w_ref[...], preferred_element_type=F32)
    tn = r.shape[1]
    if emit_kz:
        o_ref, kz_ref = outs
        y = r * cos_ref[...] + _swap_halves(r, rope) * sin_ref[...]
        o_ref[...] = y
        lane = lax.broadcasted_iota(I32, y.shape, 1)
        kz0 = jnp.where(lane < IDX_DIM, y, 0.0)
        kz_ref[:, :LANES] = kz0.astype(BF16)
        kz_ref[:, LANES:] = pltpu.roll(kz0, IDX_DIM, axis=1).astype(BF16)
        return
    if rope:
        cos = cos_ref[...]
        sin = sin_ref[...]
        for g in range(tn // LANES):
            sl = slice(g * LANES, (g + 1) * LANES)
            blk = r[:, sl]
            y = blk * cos + _swap_halves(blk, rope) * sin
            if scale != 1.0:
                y = y * scale
            for o in outs:
                o[:, sl] = y.astype(o.dtype)
        return
    if scale != 1.0:
        r = r * scale
    for o in outs:
        o[...] = r.astype(o.dtype)


def proj(x, w, *, out_dtypes, rope=None, tables=None, scale=1.0, emit_kz=False, tm=1024, tn=1024, name='proj'):
    M, K = x.shape
    _, N = w.shape
    tm, tn = min(tm, M), min(tn, N)
    assert M % tm == 0 and N % tn == 0
    in_specs = [pl.BlockSpec((tm, K), lambda n, m: (m, 0)), pl.BlockSpec((K, tn), lambda n, m: (0, n))]
    args = [x, w]
    if rope:
        in_specs += [pl.BlockSpec((tm, LANES), lambda n, m: (m, 0))] * 2
        args += list(tables)
    out_specs = [pl.BlockSpec((tm, tn), lambda n, m: (m, n)) for _ in out_dtypes]
    out_shape = [jax.ShapeDtypeStruct((M, N), dt) for dt in out_dtypes]
    if emit_kz:
        out_specs.append(pl.BlockSpec((tm, 2 * LANES), lambda n, m: (m, 0)))
        out_shape.append(jax.ShapeDtypeStruct((M, 2 * LANES), BF16))
    return pl.pallas_call(
        functools.partial(_proj_body, rope=rope, scale=scale, emit_kz=emit_kz),
        grid=(N // tn, M // tm),
        in_specs=in_specs, out_specs=out_specs, out_shape=out_shape,
        compiler_params=_params(('parallel', 'parallel')),
        name=name,
    )(*args)


def _mm_act_body(x_ref, w_ref, o_ref):
    r = jnp.dot(x_ref[...], w_ref[...], preferred_element_type=F32)
    o_ref[...] = jnp.square(jnp.maximum(r, 0.0)).astype(o_ref.dtype)


def mlp_up(x, w, *, tm, tn):
    M, K = x.shape
    _, N = w.shape
    tm, tn = min(tm, M), min(tn, N)
    return pl.pallas_call(
        _mm_act_body,
        grid=(N // tn, M // tm),
        in_specs=[pl.BlockSpec((tm, K), lambda n, m: (m, 0)), pl.BlockSpec((K, tn), lambda n, m: (0, n))],
        out_specs=pl.BlockSpec((tm, tn), lambda n, m: (m, n)),
        out_shape=jax.ShapeDtypeStruct((M, N), BF16),
        compiler_params=_params(('parallel', 'parallel')),
        name='mlp_up',
    )(x, w)


def _mm_ln_body(x_ref, w_ref, r_ref, g_ref, b_ref, o_ref, ob_ref, acc_ref, *, nk, alpha):
    k = pl.program_id(1)

    @pl.when(k == 0)
    def _():
        acc_ref[...] = jnp.zeros_like(acc_ref)

    acc_ref[...] += jnp.dot(x_ref[...], w_ref[...], preferred_element_type=F32)

    @pl.when(k == nk - 1)
    def _():
        y = alpha * r_ref[...] + acc_ref[...]
        mu = jnp.mean(y, axis=-1, keepdims=True)
        yc = y - mu
        var = jnp.mean(yc * yc, axis=-1, keepdims=True)
        out = yc * lax.rsqrt(var + LN_EPS) * g_ref[...] + b_ref[...]
        o_ref[...] = out
        ob_ref[...] = out.astype(BF16)


def matmul_ln(x, w, resid, g, b, *, alpha, tm, tk, name='mm_ln'):
    M, K = x.shape
    _, N = w.shape
    tm, tk = min(tm, M), min(tk, K)
    nk = K // tk
    return pl.pallas_call(
        functools.partial(_mm_ln_body, nk=nk, alpha=alpha),
        grid=(M // tm, nk),
        in_specs=[pl.BlockSpec((tm, tk), lambda m, k: (m, k)),
                  pl.BlockSpec((tk, N), lambda m, k: (k, 0)),
                  pl.BlockSpec((tm, N), lambda m, k: (m, 0)),
                  pl.BlockSpec((1, N), lambda m, k: (0, 0)),
                  pl.BlockSpec((1, N), lambda m, k: (0, 0))],
        out_specs=[pl.BlockSpec((tm, N), lambda m, k: (m, 0)),
                   pl.BlockSpec((tm, N), lambda m, k: (m, 0))],
        out_shape=[jax.ShapeDtypeStruct((M, N), F32), jax.ShapeDtypeStruct((M, N), BF16)],
        scratch_shapes=[pltpu.VMEM((tm, N), F32)],
        compiler_params=_params(('parallel', 'arbitrary')),
        name=name,
    )(x, w, resid, g.reshape(1, N), b.reshape(1, N))


def _branch_body(x_ref, a_ref, b_ref, c_ref, wb_ref, wga_ref, wgb_ref, wgc_ref, o_ref):
    x = x_ref[...]
    acc = None
    for n, (br, wg) in enumerate(((a_ref, wga_ref), (b_ref, wgb_ref), (c_ref, wgc_ref))):
        gate = jax.nn.sigmoid(jnp.dot(x, wg[...], preferred_element_type=F32))
        t = gate * jnp.dot(br[...], wb_ref[n], preferred_element_type=F32)
        acc = t if acc is None else acc + t
    o_ref[...] = acc.astype(o_ref.dtype)


def branch_merge(x16, brA, brB, brC, w_branch, w_gate, *, tm, tn):
    M = x16.shape[0]
    tm = min(tm, M)
    nb = D_MODEL // tn
    gspecs = [pl.BlockSpec((D_MODEL, tn), functools.partial(lambda m, n, base: (0, base + n), base=i * nb))
              for i in range(3)]
    return pl.pallas_call(
        _branch_body,
        grid=(M // tm, nb),
        in_specs=[pl.BlockSpec((tm, D_MODEL), lambda m, n: (m, 0))]
        + [pl.BlockSpec((tm, BRANCH_WIDTH), lambda m, n: (m, 0))] * 3
        + [pl.BlockSpec((3, BRANCH_WIDTH, tn), lambda m, n: (0, 0, n))] + gspecs,
        out_specs=pl.BlockSpec((tm, tn), lambda m, n: (m, n)),
        out_shape=jax.ShapeDtypeStruct((M, D_MODEL), BF16),
        compiler_params=_params(('parallel', 'parallel')),
        name='branch_merge',
    )(x16, brA, brB, brC, w_branch, w_gate, w_gate, w_gate)


def _dsa_prompt_body(qi_ref, misc_ref, kz_ref, qa_ref, ka_ref, va_ref, o_ref, keys_ref, bias_ref, *, top_k, tk, T):
    QB = Q_BLOCK
    qb = pl.program_id(1)
    q0 = qb * QB
    nkt = (q0 + QB - 1) // tk + 1
    qpos = q0 + lax.broadcasted_iota(I32, (QB, tk), 0)
    lane = lax.broadcasted_iota(I32, (QB, tk), 1)
    qpos_c = q0 + lax.broadcasted_iota(I32, (QB, LANES), 0)
    lane_c = lax.broadcasted_iota(I32, (QB, LANES), 1)
    wi = misc_ref[:, MISC_WI:MISC_WI + IDX_HEADS]

    def score_tile(j, carry):
        off = pl.multiple_of(j * tk, tk)
        kz = (kz_ref[pl.ds(off, tk), :LANES], kz_ref[pl.ds(off, tk), LANES:])
        acc = jnp.zeros((QB, tk), F32)
        for h in range(IDX_HEADS):
            qp = qi_ref[:, (h // 2) * LANES:(h // 2 + 1) * LANES]
            s = lax.dot_general(qp, kz[h % 2], NT_DIMS, preferred_element_type=F32)
            acc = acc + wi[:, h:h + 1] * jnp.maximum(s, 0.0)
        key = jnp.where(off + lane <= qpos, _sortable(acc), INT_MIN)
        keys_ref[:, pl.ds(off, tk)] = key
        return carry

    lax.fori_loop(0, nkt, score_tile, 0)

    def count(pred):
        def body(j, acc):
            off = pl.multiple_of(j * tk, tk)
            for c in range(tk // LANES):
                kc = keys_ref[:, pl.ds(off + c * LANES, LANES)]
                acc = acc + jnp.where(pred(kc, off + c * LANES + lane_c), 1.0, 0.0)
            return acc
        acc = lax.fori_loop(0, nkt, body, jnp.zeros((QB, LANES), F32))
        return jnp.sum(acc, axis=1, keepdims=True)

    def bcast(v):
        return jnp.broadcast_to(v, (QB, LANES))

    kf = float(top_k)
    c0 = count(lambda kc, kp: kc >= 0)
    lo = jnp.where(c0 >= kf, jnp.int32(0), jnp.int32(INT_MIN))

    def bit_body(i, lo):
        cand = lo + (jnp.int32(1) << (30 - i))
        cand_b = bcast(cand)
        c = count(lambda kc, kp: kc >= cand_b)
        return jnp.where(c >= kf, cand, lo)

    thr = lax.fori_loop(0, 31, bit_body, lo)
    thr_b = bcast(thr)

    need = kf - count(lambda kc, kp: kc > thr_b)
    ceq = count(lambda kc, kp: kc == thr_b)
    nbits = max(T.bit_length(), 1)

    def tie_fn():
        def jb(i, j):
            cand = j + (jnp.int32(1) << (nbits - 1 - i))
            cand_b = bcast(cand)
            g = count(lambda kc, kp: (kc == thr_b) & (kp < cand_b))
            return jnp.where(g <= need, cand, j)
        return lax.fori_loop(0, nbits, jb, jnp.zeros((QB, 1), I32))

    jstar = lax.cond(jnp.max(ceq - need) > 0.0, tie_fn, lambda: jnp.full((QB, 1), 2 ** 30, I32))
    jstar_b = bcast(jstar)

    def bias_tile(j, carry):
        off = pl.multiple_of(j * tk, tk)
        for c in range(tk // LANES):
            kc = keys_ref[:, pl.ds(off + c * LANES, LANES)]
            kp = off + c * LANES + lane_c
            sel = ((kc > thr_b) | ((kc == thr_b) & (kp < jstar_b))) & (kp <= qpos_c)
            bias_ref[:, pl.ds(off + c * LANES, LANES)] = jnp.where(sel, 0.0, NEG)
        return carry

    lax.fori_loop(0, nkt, bias_tile, 0)

    qgs = [jnp.concatenate([qa_ref[:, (A_REP * g + r) * HEAD_DIM:(A_REP * g + r + 1) * HEAD_DIM]
                            for r in range(A_REP)], axis=0) for g in range(A_KV_HEADS)]

    def att_tile(j, carry):
        off = pl.multiple_of(j * tk, tk)
        b = bias_ref[:, pl.ds(off, tk)]
        new = []
        for g in range(A_KV_HEADS):
            m, l, acc = carry[g]
            gs = slice(g * HEAD_DIM, (g + 1) * HEAD_DIM)
            kt = ka_ref[pl.ds(off, tk), gs]
            vt = va_ref[pl.ds(off, tk), gs]
            s = lax.dot_general(qgs[g], kt, NT_DIMS, preferred_element_type=F32)
            s = (s.reshape(A_REP, QB, tk) + b[None]).reshape(A_REP * QB, tk)
            m_new = jnp.maximum(m, jnp.max(s, axis=1, keepdims=True))
            corr = jnp.exp(m - m_new)
            p = jnp.exp(s - m_new)
            l = l * corr + jnp.sum(p, axis=1, keepdims=True)
            acc = acc * corr + jnp.dot(p.astype(BF16), vt, preferred_element_type=F32)
            new.append((m_new, l, acc))
        return tuple(new)

    one = (jnp.full((A_REP * QB, 1), NEG, F32), jnp.zeros((A_REP * QB, 1), F32),
           jnp.zeros((A_REP * QB, HEAD_DIM), F32))
    carry = lax.fori_loop(0, nkt, att_tile, (one,) * A_KV_HEADS)
    for g in range(A_KV_HEADS):
        _, l, acc = carry[g]
        out = acc / l
        for r in range(A_REP):
            h = A_REP * g + r
            o_ref[:, h * HEAD_DIM:(h + 1) * HEAD_DIM] = out[r * QB:(r + 1) * QB].astype(o_ref.dtype)


def dsa_prompt(qi, misc, kz, q128, k128, v16):
    B, T, _ = qi.shape
    top_k = min(TOPK_MAX, T // 4)
    tk = min(512, T)
    nb = T // Q_BLOCK
    kvw = A_KV_HEADS * HEAD_DIM
    return pl.pallas_call(
        functools.partial(_dsa_prompt_body, top_k=top_k, tk=tk, T=T),
        grid=(B, nb),
        in_specs=[pl.BlockSpec((None, Q_BLOCK, IDX_HEADS * IDX_DIM), lambda b, q: (b, q, 0)),
                  pl.BlockSpec((None, Q_BLOCK, LANES), lambda b, q: (b, q, 0)),
                  pl.BlockSpec((None, T, 2 * LANES), lambda b, q: (b, 0, 0)),
                  pl.BlockSpec((None, Q_BLOCK, A_HEADS * HEAD_DIM), lambda b, q: (b, q, 0)),
                  pl.BlockSpec((None, T, kvw), lambda b, q: (b, 0, 0)),
                  pl.BlockSpec((None, T, kvw), lambda b, q: (b, 0, 0))],
        out_specs=pl.BlockSpec((None, Q_BLOCK, A_HEADS * HEAD_DIM), lambda b, q: (b, q, 0)),
        out_shape=jax.ShapeDtypeStruct((B, T, A_HEADS * HEAD_DIM), BF16),
        scratch_shapes=[pltpu.VMEM((Q_BLOCK, T), I32), pltpu.VMEM((Q_BLOCK, T), F32)],
        compiler_params=_params(('parallel', 'arbitrary')),
        name='dsa_prompt',
    )(qi, misc, kz, q128, k128, v16)


def _lam(lq1, lk1, lq2, lk2, lam_init):
    return (jnp.exp(jnp.sum(lq1[...] * lk1[...], axis=-1, keepdims=True))
            - jnp.exp(jnp.sum(lq2[...] * lk2[...], axis=-1, keepdims=True)) + lam_init)


def _diff_finish(o0, o1, lam, subln, lam_init):
    a = o0 - lam * o1
    a = a * lax.rsqrt(jnp.mean(a * a, axis=-1, keepdims=True) + RMS_EPS) * subln
    return a * (1.0 - lam_init)


def _diff_prompt_body(lq1, lk1, lq2, lk2, sub_ref, q_ref, k_ref, v_ref, o_ref, *, tq, tk, lam_init):
    qi = pl.program_id(2)
    q0 = qi * tq
    n_full = q0 // tk
    n_all = (q0 + tq - 1) // tk + 1
    qpos = q0 + lax.broadcasted_iota(I32, (tq, tk), 0)
    lane = lax.broadcasted_iota(I32, (tq, tk), 1)
    qs = [q_ref[:, mp * HEAD_DIM:(mp + 1) * HEAD_DIM] for mp in range(2)]

    def tile(j, carry, masked):
        off = pl.multiple_of(j * tk, tk)
        vt = v_ref[pl.ds(off, tk), :]
        new = []
        for mp in range(2):
            m, l, acc = carry[mp]
            kt = k_ref[pl.ds(off, tk), mp * HEAD_DIM:(mp + 1) * HEAD_DIM]
            s = lax.dot_general(qs[mp], kt, NT_DIMS, preferred_element_type=F32)
            if masked:
                s = jnp.where(off + lane <= qpos, s, NEG)
            m_new = jnp.maximum(m, jnp.max(s, axis=1, keepdims=True))
            corr = jnp.exp(m - m_new)
            p = jnp.exp(s - m_new)
            l = l * corr + jnp.sum(p, axis=1, keepdims=True)
            acc = acc * corr + jnp.dot(p.astype(BF16), vt, preferred_element_type=F32)
            new.append((m_new, l, acc))
        return tuple(new)

    one = (jnp.full((tq, 1), NEG, F32), jnp.zeros((tq, 1), F32), jnp.zeros((tq, B_V_DIM), F32))
    carry = lax.fori_loop(0, n_full, functools.partial(tile, masked=False), (one, one))
    carry = lax.fori_loop(n_full, n_all, functools.partial(tile, masked=True), carry)
    outs = [acc / l for (_, l, acc) in carry]
    lam = _lam(lq1, lk1, lq2, lk2, lam_init)
    o_ref[...] = _diff_finish(outs[0], outs[1], lam, sub_ref[...], lam_init).astype(o_ref.dtype)


def diff_prompt(q128, k128, v16, lam_q1, lam_k1, lam_q2, lam_k2, subln, lam_init):
    B, T, _ = q128.shape
    H = B_HEADS
    tq = min(512, T)
    tk = min(512, T)
    pw = 2 * HEAD_DIM
    qb0 = (A_HEADS * HEAD_DIM) // pw
    kb0 = (A_KV_HEADS * HEAD_DIM) // pw
    vec = pl.BlockSpec((1, HEAD_DIM), lambda b, h, q: (0, 0))
    return pl.pallas_call(
        functools.partial(_diff_prompt_body, tq=tq, tk=tk, lam_init=lam_init),
        grid=(B, H, T // tq),
        in_specs=[vec, vec, vec, vec,
                  pl.BlockSpec((1, B_V_DIM), lambda b, h, q: (0, 0)),
                  pl.BlockSpec((None, tq, pw), lambda b, h, q: (b, q, qb0 + h)),
                  pl.BlockSpec((None, T, pw), lambda b, h, q: (b, 0, kb0 + h)),
                  pl.BlockSpec((None, T, B_V_DIM), lambda b, h, q: (b, 0, kb0 + h))],
        out_specs=pl.BlockSpec((None, tq, B_V_DIM), lambda b, h, q: (b, q, h)),
        out_shape=jax.ShapeDtypeStruct((B, T, H * B_V_DIM), BF16),
        compiler_params=_params(('parallel', 'parallel', 'arbitrary')),
        name='diff_prompt',
    )(lam_q1.reshape(1, -1), lam_k1.reshape(1, -1), lam_q2.reshape(1, -1), lam_k2.reshape(1, -1),
      subln.reshape(1, -1), q128, k128, v16)


def _gla_body(q_ref, k_ref, v_ref, rc_ref, misc_ref, w2_ref, gb_ref, gn_ref, s0_ref, o_ref, sfin_ref, st_ref,
              *, tb, chunk, t_valid, nt):
    t = pl.program_id(1)

    @pl.when(t == 0)
    def _():
        st_ref[...] = s0_ref[...]

    x = jnp.dot(misc_ref[...].astype(BF16), w2_ref[...], preferred_element_type=F32) + gb_ref[...]
    la = (jnp.minimum(x, 0.0) - jnp.log(1.0 + jnp.exp(-jnp.abs(x)))) * (1.0 / GLA_TAU)
    W = C_HEADS * C_K_DIM
    row = lax.broadcasted_iota(I32, (tb, W), 0)
    if t_valid is not None:
        la = jnp.where(t * tb + row < t_valid, la, 0.0)
    rowc = row % chunk
    b = la
    sh = 1
    while sh < chunk:
        b = b + jnp.where(rowc >= sh, pltpu.roll(b, sh, axis=0), 0.0)
        sh *= 2
    k = k_ref[...]
    qe = (q_ref[...] * (C_K_DIM ** -0.5) * jnp.exp(b)).astype(BF16)
    ke = (k * jnp.exp(-b)).astype(BF16)
    v16 = v_ref[...].astype(BF16)
    rc = rc_ref[...]
    gn = gn_ref[...]
    tril = lax.broadcasted_iota(I32, (chunk, chunk), 0) >= lax.broadcasted_iota(I32, (chunk, chunk), 1)
    for c in range(tb // chunk):
        r0 = c * chunk
        bl = b[r0 + chunk - 1:r0 + chunk, :]
        kd = (k[r0:r0 + chunk] * jnp.exp(bl - b[r0:r0 + chunk])).astype(BF16)
        dec = jnp.exp(bl)
        for h in range(C_HEADS):
            ck = slice(h * C_K_DIM, (h + 1) * C_K_DIM)
            cv = slice(h * C_V_DIM, (h + 1) * C_V_DIM)
            qe_c = qe[r0:r0 + chunk, ck]
            v_c = v16[r0:r0 + chunk, cv]
            att = lax.dot_general(qe_c, ke[r0:r0 + chunk, ck], NT_DIMS, preferred_element_type=F32)
            att = jnp.where(tril, att, 0.0).astype(BF16)
            st = st_ref[h]
            o_c = (lax.dot_general(qe_c, st.astype(BF16), NT_DIMS, preferred_element_type=F32)
                   + jnp.dot(att, v_c, preferred_element_type=F32))
            st_ref[h] = st * dec[:, ck] + lax.dot_general(v_c, kd[:, ck], TN_DIMS, preferred_element_type=F32)
            o_n = o_c * lax.rsqrt(jnp.mean(o_c * o_c, axis=-1, keepdims=True) + RMS_EPS) * gn
            r_c = rc[r0:r0 + chunk, cv]
            o_ref[r0:r0 + chunk, cv] = (o_n * (r_c * jax.nn.sigmoid(r_c))).astype(o_ref.dtype)

    @pl.when(t == nt - 1)
    def _():
        sfin_ref[...] = st_ref[...]


def gla(cproj, misc, w2pad, gb, gn, s0t, *, t_valid=None):
    B, T, _ = cproj.shape
    chunk = GLA_CHUNK
    tb = min(256, T)
    nt = T // tb
    W = C_HEADS * C_K_DIM
    V = C_HEADS * C_V_DIM
    st_spec = pl.BlockSpec((None, C_HEADS, C_V_DIM, C_K_DIM), lambda b, t: (b, 0, 0, 0))
    return pl.pallas_call(
        functools.partial(_gla_body, tb=tb, chunk=chunk, t_valid=t_valid, nt=nt),
        grid=(B, nt),
        in_specs=[pl.BlockSpec((None, tb, W), lambda b, t: (b, t, 0)),
                  pl.BlockSpec((None, tb, W), lambda b, t: (b, t, 1)),
                  pl.BlockSpec((None, tb, V), lambda b, t: (b, t, 1)),
                  pl.BlockSpec((None, tb, V), lambda b, t: (b, t, 2)),
                  pl.BlockSpec((None, tb, LANES), lambda b, t: (b, t, 0)),
                  pl.BlockSpec((LANES, W), lambda b, t: (0, 0)),
                  pl.BlockSpec((1, W), lambda b, t: (0, 0)),
                  pl.BlockSpec((1, C_V_DIM), lambda b, t: (0, 0)),
                  st_spec],
        out_specs=[pl.BlockSpec((None, tb, V), lambda b, t: (b, t, 0)), st_spec],
        out_shape=[jax.ShapeDtypeStruct((B, T, V), BF16),
                   jax.ShapeDtypeStruct((B, C_HEADS, C_V_DIM, C_K_DIM), F32)],
        scratch_shapes=[pltpu.VMEM((C_HEADS, C_V_DIM, C_K_DIM), F32)],
        compiler_params=_params(('parallel', 'arbitrary')),
        name='gla',
    )(cproj, cproj, cproj, cproj, misc, w2pad, gb.reshape(1, W), gn.reshape(1, C_V_DIM), s0t)


def _page_specs(block, layer, pps):
    def mk(i):
        return pl.BlockSpec((None, None) + block,
                            lambda b, p, pt: (layer, pt[b, p * pps + i]) + (0,) * len(block))
    return [mk(i) for i in range(pps)]


def _dsa_sample_score_body(pt_ref, qi_ref, wi_ref, kin_ref, *rest, pps, n_pages):
    pages, o_ref = rest[:pps], rest[pps]
    p = pl.program_id(1)
    qi = qi_ref[...]
    wi = wi_ref[...]

    @pl.when(p == 0)
    def _():
        o_ref[...] = jnp.full(o_ref.shape, -jnp.inf, F32)
        s = jnp.sum(qi.astype(F32) * kin_ref[...], axis=1, keepdims=True)
        snew = jnp.sum(wi * jnp.maximum(s, 0.0), axis=0, keepdims=True)
        lane = lax.broadcasted_iota(I32, (1, PAGE_SIZE), 1)
        o_ref[n_pages:n_pages + 1, :] = jnp.where(lane == 0, snew, -jnp.inf)

    for i in range(pps):
        kpt = pages[i][...].astype(BF16)
        s = jnp.dot(qi, kpt, preferred_element_type=F32)
        o_ref[pl.ds(p * pps + i, 1), :] = jnp.sum(wi * jnp.maximum(s, 0.0), axis=0, keepdims=True)


def _topk_bias(sc, top_k, n_valid):
    R = sc.shape[0]
    key = _sortable(sc)
    pos = lax.broadcasted_iota(I32, sc.shape, 0) * PAGE_SIZE + lax.broadcasted_iota(I32, sc.shape, 1)
    key = jnp.where(pos < n_valid, key, INT_MIN)

    def count(hit):
        c = jnp.sum(jnp.where(hit, 1.0, 0.0), axis=1, keepdims=True)
        return jnp.sum(c, axis=0, keepdims=True)

    kf = float(top_k)
    lo = jnp.where(count(key >= 0) >= kf, jnp.int32(0), jnp.int32(INT_MIN))

    def bit_body(i, lo):
        cand = lo + (jnp.int32(1) << (30 - i))
        return jnp.where(count(key >= cand) >= kf, cand, lo)

    thr = lax.fori_loop(0, 31, bit_body, lo)
    need = kf - count(key > thr)
    nbits = (R * PAGE_SIZE).bit_length()

    def jb(i, j):
        cand = j + (jnp.int32(1) << (nbits - 1 - i))
        return jnp.where(count((key == thr) & (pos < cand)) <= need, cand, j)

    jstar = lax.fori_loop(0, nbits, jb, jnp.zeros((1, 1), I32))
    sel = ((key > thr) | ((key == thr) & (pos < jstar))) & (pos < n_valid)
    return jnp.where(sel, 0.0, NEG)


def _dsa_sample_attn_body(pt_ref, sc_ref, q_ref, kn_ref, vn_ref, *rest, pps, n_pages, top_k):
    kpages, vpages = rest[:pps], rest[pps:2 * pps]
    o_ref, bias_ref, m_ref, l_ref, acc_ref = rest[2 * pps:]
    p = pl.program_id(1)
    q = q_ref[...]
    first = lax.broadcasted_iota(I32, (A_HEADS, PAGE_SIZE), 0) < A_REP

    @pl.when(p == 0)
    def _():
        bias_ref[...] = _topk_bias(sc_ref[...], top_k, n_pages * PAGE_SIZE + 1)
        m_ref[...] = jnp.full(m_ref.shape, NEG, F32)
        l_ref[...] = jnp.zeros_like(l_ref)
        acc_ref[...] = jnp.zeros_like(acc_ref)

    def kv(refs, g):
        return jnp.concatenate([r[pl.ds(g, PAGE_SIZE, stride=A_KV_HEADS), :].astype(BF16) for r in refs], axis=0)

    W = pps * PAGE_SIZE
    first_w = lax.broadcasted_iota(I32, (A_HEADS, W), 0) < A_REP
    s0 = lax.dot_general(q, kv(kpages, 0), NT_DIMS, preferred_element_type=F32)
    s1 = lax.dot_general(q, kv(kpages, 1), NT_DIMS, preferred_element_type=F32)
    bias = jnp.concatenate([bias_ref[pl.ds(p * pps + i, 1), :] for i in range(pps)], axis=1)
    s = jnp.where(first_w, s0, s1) + bias
    m = m_ref[...]
    m_new = jnp.maximum(m, jnp.max(s, axis=1, keepdims=True))
    corr = jnp.exp(m - m_new)
    pr = jnp.exp(s - m_new)
    l_ref[...] = l_ref[...] * corr + jnp.sum(pr, axis=1, keepdims=True)
    pb = pr.astype(BF16)
    pv = jnp.where(first, jnp.dot(pb, kv(vpages, 0), preferred_element_type=F32),
                   jnp.dot(pb, kv(vpages, 1), preferred_element_type=F32))
    acc_ref[...] = acc_ref[...] * corr + pv
    m_ref[...] = m_new

    @pl.when(p == pl.num_programs(1) - 1)
    def _():
        s = (jnp.sum(q.astype(F32) * kn_ref[...], axis=1, keepdims=True)
             + bias_ref[n_pages:n_pages + 1, 0:1])
        m = m_ref[...]
        m_new = jnp.maximum(m, s)
        corr = jnp.exp(m - m_new)
        pr = jnp.exp(s - m_new)
        l = l_ref[...] * corr + pr
        o_ref[...] = (acc_ref[...] * corr + pr * vn_ref[...]) / l


def dsa_sample(layer, page_table, qi, wi, ki_new, qa, ka_new, va_new, cache_idx_kt, cache_a_k, cache_a_v):
    DB, n_pages = page_table.shape
    pps = math.gcd(8, n_pages)
    R = ((n_pages + 1 + 7) // 8) * 8
    L = n_pages * PAGE_SIZE + 1
    top_k = min(TOPK_MAX, L // 4)
    per_b = lambda *blk: pl.BlockSpec((None,) + blk, lambda b, p, pt: (b,) + (0,) * len(blk))
    scores = pl.pallas_call(
        functools.partial(_dsa_sample_score_body, pps=pps, n_pages=n_pages),
        grid_spec=pltpu.PrefetchScalarGridSpec(
            num_scalar_prefetch=1, grid=(DB, n_pages // pps),
            in_specs=[per_b(IDX_HEADS, IDX_DIM), per_b(IDX_HEADS, 1), per_b(1, IDX_DIM)]
            + _page_specs((IDX_DIM, PAGE_SIZE), layer, pps),
            out_specs=per_b(R, PAGE_SIZE)),
        out_shape=jax.ShapeDtypeStruct((DB, R, PAGE_SIZE), F32),
        compiler_params=_params(('parallel', 'arbitrary')),
        name='dsa_sample_scores',
    )(page_table, qi, wi, ki_new, *([cache_idx_kt] * pps))
    rows = PAGE_SIZE * A_KV_HEADS
    return pl.pallas_call(
        functools.partial(_dsa_sample_attn_body, pps=pps, n_pages=n_pages, top_k=top_k),
        grid_spec=pltpu.PrefetchScalarGridSpec(
            num_scalar_prefetch=1, grid=(DB, n_pages // pps),
            in_specs=[per_b(R, PAGE_SIZE), per_b(A_HEADS, HEAD_DIM), per_b(A_HEADS, HEAD_DIM), per_b(A_HEADS, HEAD_DIM)]
            + _page_specs((rows, HEAD_DIM), layer, pps) + _page_specs((rows, HEAD_DIM), layer, pps),
            out_specs=per_b(A_HEADS, HEAD_DIM),
            scratch_shapes=[pltpu.VMEM((R, PAGE_SIZE), F32), pltpu.VMEM((A_HEADS, 1), F32),
                            pltpu.VMEM((A_HEADS, 1), F32), pltpu.VMEM((A_HEADS, HEAD_DIM), F32)]),
        out_shape=jax.ShapeDtypeStruct((DB, A_HEADS, HEAD_DIM), F32),
        compiler_params=_params(('parallel', 'arbitrary')),
        name='dsa_sample_attn',
    )(page_table, scores, qa, ka_new, va_new, *([cache_a_k] * pps), *([cache_a_v] * pps))


def _diff_sample_body(pt_ref, lq1, lk1, lq2, lk2, sub_ref, q_ref, kn_ref, vn_ref, *rest, pps, lam_init):
    kpages, vpages = rest[:pps], rest[pps:2 * pps]
    o_ref, m_ref, l_ref, acc_ref = rest[2 * pps:]
    p = pl.program_id(1)
    NJ = 2 * B_HEADS
    q = q_ref[...]
    rowk = lax.broadcasted_iota(I32, (NJ, PAGE_SIZE), 0)

    @pl.when(p == 0)
    def _():
        m_ref[...] = jnp.full(m_ref.shape, NEG, F32)
        l_ref[...] = jnp.zeros_like(l_ref)
        acc_ref[...] = jnp.zeros_like(acc_ref)

    def rows(refs, j):
        return jnp.concatenate([r[pl.ds(j, PAGE_SIZE, stride=NJ), :].astype(BF16) for r in refs], axis=0)

    W = pps * PAGE_SIZE
    roww = lax.broadcasted_iota(I32, (NJ, W), 0)
    s = jnp.zeros((NJ, W), F32)
    for j in range(NJ):
        sj = lax.dot_general(q, rows(kpages, j), NT_DIMS, preferred_element_type=F32)
        s = jnp.where(roww == j, sj, s)
    m = m_ref[...]
    m_new = jnp.maximum(m, jnp.max(s, axis=1, keepdims=True))
    corr = jnp.exp(m - m_new)
    pr = jnp.exp(s - m_new)
    l_ref[...] = l_ref[...] * corr + jnp.sum(pr, axis=1, keepdims=True)
    pb = pr.astype(BF16)
    halves = []
    for c in range(B_V_DIM // LANES):
        pv = jnp.zeros((NJ, LANES), F32)
        for h in range(B_HEADS):
            ph = jnp.dot(pb, rows(vpages, c * B_HEADS + h), preferred_element_type=F32)
            pv = jnp.where(rowk // 2 == h, ph, pv)
        halves.append(pv)
    acc_ref[...] = acc_ref[...] * corr + jnp.concatenate(halves, axis=1)
    m_ref[...] = m_new

    @pl.when(p == pl.num_programs(1) - 1)
    def _():
        s = jnp.sum(q.astype(F32) * kn_ref[...], axis=1, keepdims=True)
        m = m_ref[...]
        m_new = jnp.maximum(m, s)
        corr = jnp.exp(m - m_new)
        pr = jnp.exp(s - m_new)
        l = l_ref[...] * corr + pr
        o = (acc_ref[...] * corr + pr * vn_ref[...]) / l
        lam = _lam(lq1, lk1, lq2, lk2, lam_init)
        for h in range(B_HEADS):
            o_ref[h:h + 1, :] = _diff_finish(o[2 * h:2 * h + 1], o[2 * h + 1:2 * h + 2], lam, sub_ref[...], lam_init)


def diff_sample(layer, page_table, qb, kb_new, vb_new, cache_b_k, cache_b_v,
                lam_q1, lam_k1, lam_q2, lam_k2, subln, lam_init):
    DB, n_pages = page_table.shape
    pps = math.gcd(8, n_pages)
    NJ = 2 * B_HEADS
    per_b = lambda *blk: pl.BlockSpec((None,) + blk, lambda b, p, pt: (b,) + (0,) * len(blk))
    vec = lambda w: pl.BlockSpec((1, w), lambda b, p, pt: (0, 0))
    return pl.pallas_call(
        functools.partial(_diff_sample_body, pps=pps, lam_init=lam_init),
        grid_spec=pltpu.PrefetchScalarGridSpec(
            num_scalar_prefetch=1, grid=(DB, n_pages // pps),
            in_specs=[vec(HEAD_DIM)] * 4 + [vec(B_V_DIM),
                      per_b(NJ, HEAD_DIM), per_b(NJ, HEAD_DIM), per_b(NJ, B_V_DIM)]
            + _page_specs((PAGE_SIZE * NJ, HEAD_DIM), layer, pps)
            + _page_specs((PAGE_SIZE * NJ, LANES), layer, pps),
            out_specs=per_b(B_HEADS, B_V_DIM),
            scratch_shapes=[pltpu.VMEM((NJ, 1), F32), pltpu.VMEM((NJ, 1), F32), pltpu.VMEM((NJ, B_V_DIM), F32)]),
        out_shape=jax.ShapeDtypeStruct((DB, B_HEADS, B_V_DIM), F32),
        compiler_params=_params(('parallel', 'arbitrary')),
        name='diff_sample',
    )(page_table, lam_q1.reshape(1, -1), lam_k1.reshape(1, -1), lam_q2.reshape(1, -1), lam_k2.reshape(1, -1),
      subln.reshape(1, -1), qb, kb_new, vb_new, *([cache_b_k] * pps), *([cache_b_v] * pps))


def _split_w_in(w):
    src = {}
    off = 0
    for name, width in SRC_SEGMENTS:
        src[name] = w[:, off:off + width]
        off += width
    cat = lambda *names: jnp.concatenate([src[n] for n in names], axis=1).astype(BF16)
    pad = jnp.zeros((w.shape[0], LANES - IDX_DIM - IDX_HEADS - GLA_RANK), w.dtype)
    return {
        'q128': cat('qa', 'qb'), 'k128': cat('ka', 'kb'), 'qi': cat('qi'), 'v': cat('va', 'vb'),
        'c': cat('qc', 'kc', 'vc', 'rc'), 'gate': cat('gate'),
        'misc': jnp.concatenate([src['ki'], src['wi'], src['gc'], pad], axis=1).astype(BF16),
    }


def _rope_tables(pos):
    def tab(dh):
        half = dh // 2
        inv_freq = ROPE_THETA ** (-jnp.arange(half, dtype=F32) / half)
        ang = pos.astype(F32)[:, None] * inv_freq[None, :]
        c, s = jnp.cos(ang), jnp.sin(ang)
        reps = LANES // dh
        return jnp.tile(jnp.concatenate([c, c], axis=1), (1, reps)), jnp.tile(jnp.concatenate([-s, s], axis=1), (1, reps))
    c128, s128 = tab(HEAD_DIM)
    c64, s64 = tab(IDX_DIM)
    n = pos.shape[0]
    tail_c = jnp.concatenate([jnp.full((n, IDX_HEADS), IDX_HEADS ** -0.5, F32),
                              jnp.ones((n, LANES - IDX_DIM - IDX_HEADS), F32)], axis=1)
    cm = jnp.concatenate([c64[:, :IDX_DIM], tail_c], axis=1)
    sm = jnp.concatenate([s64[:, :IDX_DIM], jnp.zeros((n, LANES - IDX_DIM), F32)], axis=1)
    return (c128, s128), (c64, s64), (cm, sm)


def _project(x16, wg, tabs):
    t128, t64, tm_ = tabs
    q128, = proj(x16, wg['q128'], out_dtypes=(BF16,), rope=HEAD_DIM, tables=t128, scale=HEAD_DIM ** -0.5, name='proj_q128')
    k128f, k128 = proj(x16, wg['k128'], out_dtypes=(F32, BF16), rope=HEAD_DIM, tables=t128, tn=1280, name='proj_k128')
    qi, = proj(x16, wg['qi'], out_dtypes=(BF16,), rope=IDX_DIM, tables=t64, scale=IDX_DIM ** -0.5, name='proj_qi')
    misc, kz = proj(x16, wg['misc'], out_dtypes=(F32,), rope=IDX_DIM, tables=tm_, emit_kz=True, name='proj_misc')
    vf, v16 = proj(x16, wg['v'], out_dtypes=(F32, BF16), tn=1280, name='proj_v')
    cproj, = proj(x16, wg['c'], out_dtypes=(F32,), name='proj_c')
    return dict(q128=q128, k128f=k128f, k128=k128, qi=qi, misc=misc, kz=kz, vf=vf, v16=v16, c=cproj)


def _tail(x, x16, oA, oB, oC, lw, alpha):
    g = branch_merge(x16, oA, oB, oC, lw['w_branch'], lw['w_gate'], tm=1024, tn=512)
    x1, x1b = matmul_ln(g, lw['w_out'], x, lw['ln1_g'], lw['ln1_b'], alpha=alpha, tm=512, tk=1024, name='out_ln1')
    hid = mlp_up(x1b, lw['w_up'], tm=1024, tn=1024)
    return matmul_ln(hid, lw['w_down'], x1, lw['ln2_g'], lw['ln2_b'], alpha=alpha, tm=512, tk=1024, name='down_ln2')


def kernel(x_prompt, x_sample, cache_a_k, cache_a_v, cache_idx_k, cache_b_k, cache_b_v, state_gla, page_table,
           w_in, gla_w2, gla_b, lam_q1, lam_k1, lam_q2, lam_k2, diff_subln, gla_norm, w_branch, w_out,
           ln1_g, ln1_b, w_up, w_down, ln2_g, ln2_b):
    B, T, D = x_prompt.shape
    DB, Ts, _ = x_sample.shape
    assert Ts == 1
    DEPTH = w_in.shape[0]
    n_pages = page_table.shape[1]
    P = n_pages * PAGE_SIZE
    alpha = (2.0 * DEPTH) ** 0.25
    n_phys = cache_a_k.shape[1]
    ca_k = cache_a_k.reshape(DEPTH, n_phys, PAGE_SIZE * A_KV_HEADS, HEAD_DIM)
    ca_v = cache_a_v.reshape(DEPTH, n_phys, PAGE_SIZE * A_KV_HEADS, HEAD_DIM)
    cb_k = cache_b_k.reshape(DEPTH, n_phys, PAGE_SIZE * B_HEADS * 2, HEAD_DIM)
    cb_v = jnp.transpose(cache_b_v.reshape(DEPTH, n_phys, PAGE_SIZE, B_HEADS, B_V_DIM // LANES, LANES),
                         (0, 1, 2, 4, 3, 5)).reshape(DEPTH, n_phys, PAGE_SIZE * B_HEADS * 2, LANES)
    ci_kt = jnp.swapaxes(cache_idx_k, 2, 3)
    tabs_p = _rope_tables(jnp.tile(jnp.arange(T), B))
    tabs_s = _rope_tables(jnp.tile(P + jnp.arange(Ts), DB))
    TS_PAD = GLA_CHUNK
    AKV = A_KV_HEADS * HEAD_DIM

    xp = x_prompt.reshape(B * T, D)
    xs = x_sample.reshape(DB * Ts, D)
    xp16, xs16 = xp.astype(BF16), xs.astype(BF16)
    outs = {k: [] for k in ('a_k_p', 'a_k_s', 'a_v_p', 'a_v_s', 'i_k_p', 'i_k_s', 'b_k_p', 'b_k_s',
                            'b_v_p', 'b_v_s', 'g_p', 'g_s')}
    for l in range(DEPTH):
        lam_init = 0.8 - 0.6 * math.exp(-0.3 * l)
        wg = _split_w_in(w_in[l])
        lw = {'w_branch': w_branch[l].astype(BF16), 'w_out': w_out[l].astype(BF16), 'w_up': w_up[l].astype(BF16),
              'w_down': w_down[l].astype(BF16), 'w_gate': wg['gate'], 'ln1_g': ln1_g[l], 'ln1_b': ln1_b[l],
              'ln2_g': ln2_g[l], 'ln2_b': ln2_b[l]}
        w2pad = jnp.zeros((LANES, C_HEADS * C_K_DIM), F32).at[MISC_GC:MISC_GC + GLA_RANK].set(gla_w2[l]).astype(BF16)
        lam_args = (lam_q1[l], lam_k1[l], lam_q2[l], lam_k2[l], diff_subln[l], lam_init)

        pr = _project(xp16, wg, tabs_p)
        r3 = lambda a: a.reshape(B, T, a.shape[-1])
        oA = dsa_prompt(r3(pr['qi']), r3(pr['misc']), r3(pr['kz']), r3(pr['q128']), r3(pr['k128']), r3(pr['v16']))
        oB = diff_prompt(r3(pr['q128']), r3(pr['k128']), r3(pr['v16']), *lam_args)
        oC, Sp = gla(r3(pr['c']), r3(pr['misc']), w2pad, gla_b[l], gla_norm[l],
                     jnp.zeros((B, C_HEADS, C_V_DIM, C_K_DIM), F32))
        xp, xp16 = _tail(xp, xp16, oA.reshape(B * T, -1), oB.reshape(B * T, -1), oC.reshape(B * T, -1), lw, alpha)
        outs['a_k_p'].append(pr['k128f'][:, :AKV].reshape(B, T, A_KV_HEADS, HEAD_DIM))
        outs['a_v_p'].append(pr['vf'][:, :AKV].reshape(B, T, A_KV_HEADS, HEAD_DIM))
        outs['i_k_p'].append(pr['misc'][:, :IDX_DIM].reshape(B, T, IDX_DIM))
        outs['b_k_p'].append(pr['k128f'][:, AKV:].reshape(B, T, B_HEADS, 2, HEAD_DIM))
        outs['b_v_p'].append(pr['vf'][:, AKV:].reshape(B, T, B_HEADS, B_V_DIM))
        outs['g_p'].append(jnp.swapaxes(Sp, -1, -2))

        ps = _project(xs16, wg, tabs_s)
        ka_new = ps['k128f'][:, :AKV].reshape(DB, A_KV_HEADS, HEAD_DIM)
        va_new = ps['vf'][:, :AKV].reshape(DB, A_KV_HEADS, HEAD_DIM)
        kb_new = ps['k128f'][:, AKV:].reshape(DB, 2 * B_HEADS, HEAD_DIM)
        vb_new = ps['vf'][:, AKV:].reshape(DB, B_HEADS, B_V_DIM)
        ki_new = ps['misc'][:, :IDX_DIM]
        oA = dsa_sample(l, page_table,
                        ps['qi'].reshape(DB, IDX_HEADS, IDX_DIM),
                        ps['misc'][:, MISC_WI:MISC_WI + IDX_HEADS].reshape(DB, IDX_HEADS, 1),
                        ki_new.reshape(DB, 1, IDX_DIM),
                        ps['q128'][:, :A_HEADS * HEAD_DIM].reshape(DB, A_HEADS, HEAD_DIM),
                        jnp.repeat(ka_new, A_REP, axis=1), jnp.repeat(va_new, A_REP, axis=1),
                        ci_kt, ca_k, ca_v)
        oB = diff_sample(l, page_table, ps['q128'][:, A_HEADS * HEAD_DIM:].reshape(DB, 2 * B_HEADS, HEAD_DIM),
                         kb_new, jnp.repeat(vb_new, 2, axis=1), cb_k, cb_v, *lam_args)
        padt = lambda a: jnp.pad(a.reshape(DB, Ts, -1), ((0, 0), (0, TS_PAD - Ts), (0, 0)))
        oC, Ss = gla(padt(ps['c']), padt(ps['misc']), w2pad, gla_b[l], gla_norm[l],
                     jnp.swapaxes(state_gla[l], -1, -2), t_valid=Ts)
        xs, xs16 = _tail(xs, xs16, oA.reshape(DB, -1).astype(BF16), oB.reshape(DB, -1).astype(BF16),
                         oC[:, :Ts].reshape(DB * Ts, -1), lw, alpha)
        outs['a_k_s'].append(ka_new.reshape(DB, Ts, A_KV_HEADS, HEAD_DIM))
        outs['a_v_s'].append(va_new.reshape(DB, Ts, A_KV_HEADS, HEAD_DIM))
        outs['i_k_s'].append(ki_new.reshape(DB, Ts, IDX_DIM))
        outs['b_k_s'].append(kb_new.reshape(DB, Ts, B_HEADS, 2, HEAD_DIM))
        outs['b_v_s'].append(vb_new.reshape(DB, Ts, B_HEADS, B_V_DIM))
        outs['g_s'].append(jnp.swapaxes(Ss, -1, -2))

    st = {k: jnp.stack(v) for k, v in outs.items()}
    return (xp.reshape(B, T, D), xs.reshape(DB, Ts, D),
            st['a_k_p'], st['a_k_s'], st['a_v_p'], st['a_v_s'], st['i_k_p'], st['i_k_s'],
            st['b_k_p'], st['b_k_s'], st['b_v_p'], st['b_v_s'], st['g_p'], st['g_s'])
```

```python
import functools
import math

import jax
import jax.numpy as jnp
from jax import lax
from jax.experimental import pallas as pl
from jax.experimental.pallas import tpu as pltpu

F32 = jnp.float32
BF16 = jnp.bfloat16
I32 = jnp.int32

LANES = 128
HEAD_DIM = 128
A_HEADS = 8
A_KV_HEADS = 2
A_REP = A_HEADS // A_KV_HEADS
IDX_HEADS = 16
IDX_DIM = 64
TOPK_MAX = 256
B_HEADS = 4
B_V_DIM = 256
C_HEADS = 4
C_V_DIM = 256
C_K_DIM = 128
GLA_RANK = 16
GLA_TAU = 16.0
GLA_CHUNK = 32
PAGE_SIZE = 128
Q_BLOCK = 128
ROPE_THETA = 10000.0
LN_EPS = 1e-5
RMS_EPS = 1e-6
BRANCH_WIDTH = 1024
D_MODEL = 2048

NEG = -1e30
INT_MIN = -(2 ** 31)
VMEM_LIMIT = 56 * 1024 * 1024

NT_DIMS = (((1,), (1,)), ((), ()))
TN_DIMS = (((0,), (0,)), ((), ()))

MISC_WI = IDX_DIM
MISC_GC = IDX_DIM + IDX_HEADS

SRC_SEGMENTS = (
    ('qa', 1024), ('ka', 256), ('va', 256), ('qi', 1024), ('ki', 64), ('wi', 16),
    ('qb', 1024), ('kb', 1024), ('vb', 1024), ('qc', 512), ('kc', 512), ('vc', 1024),
    ('gc', 16), ('rc', 1024), ('gate', 6144),
)


def _params(sem):
    return pltpu.CompilerParams(dimension_semantics=sem, vmem_limit_bytes=VMEM_LIMIT)


def _sortable(x):
    bits = lax.bitcast_convert_type(x, I32)
    return jnp.where(bits < 0, bits ^ jnp.int32(0x7FFFFFFF), bits)


def _swap_halves(blk, dh):
    if dh == LANES:
        return pltpu.roll(blk, LANES // 2, axis=1)
    lane = lax.broadcasted_iota(I32, blk.shape, 1)
    half = dh // 2
    return jnp.where(lane % dh < half, pltpu.roll(blk, LANES - half, axis=1), pltpu.roll(blk, half, axis=1))


def _proj_body(x_ref, w_ref, *rest, rope, scale, emit_kz):
    if rope:
        cos_ref, sin_ref = rest[:2]
        outs = rest[2:]
    else:
        outs = rest
    r = jnp.dot(x_ref[...], w_ref[...], preferred_element_type=F32)
    tn = r.shape[1]
    if emit_kz:
        o_ref, kz_ref = outs
        y = r * cos_ref[...] + _swap_halves(r, rope) * sin_ref[...]
        o_ref[...] = y
        lane = lax.broadcasted_iota(I32, y.shape, 1)
        kz0 = jnp.where(lane < IDX_DIM, y, 0.0)
        kz_ref[:, :LANES] = kz0.astype(BF16)
        kz_ref[:, LANES:] = pltpu.roll(kz0, IDX_DIM, axis=1).astype(BF16)
        return
    if rope:
        cos = cos_ref[...]
        sin = sin_ref[...]
        for g in range(tn // LANES):
            sl = slice(g * LANES, (g + 1) * LANES)
            blk = r[:, sl]
            y = blk * cos + _swap_halves(blk, rope) * sin
            if scale != 1.0:
                y = y * scale
            for o in outs:
                o[:, sl] = y.astype(o.dtype)
        return
    if scale != 1.0:
        r = r * scale
    for o in outs:
        o[...] = r.astype(o.dtype)


def proj(x, w, *, out_dtypes, rope=None, tables=None, scale=1.0, emit_kz=False, tm=1024, tn=1024, name='proj'):
    M, K = x.shape
    _, N = w.shape
    tm, tn = min(tm, M), min(tn, N)
    assert M % tm == 0 and N % tn == 0
    in_specs = [pl.BlockSpec((tm, K), lambda n, m: (m, 0)), pl.BlockSpec((K, tn), lambda n, m: (0, n))]
    args = [x, w]
    if rope:
        in_specs += [pl.BlockSpec((tm, LANES), lambda n, m: (m, 0))] * 2
        args += list(tables)
    out_specs = [pl.BlockSpec((tm, tn), lambda n, m: (m, n)) for _ in out_dtypes]
    out_shape = [jax.ShapeDtypeStruct((M, N), dt) for dt in out_dtypes]
    if emit_kz:
        out_specs.append(pl.BlockSpec((tm, 2 * LANES), lambda n, m: (m, 0)))
        out_shape.append(jax.ShapeDtypeStruct((M, 2 * LANES), BF16))
    return pl.pallas_call(
        functools.partial(_proj_body, rope=rope, scale=scale, emit_kz=emit_kz),
        grid=(N // tn, M // tm),
        in_specs=in_specs, out_specs=out_specs, out_shape=out_shape,
        compiler_params=_params(('parallel', 'parallel')),
        name=name,
    )(*args)


def _proj_cache_body(x_ref, w_ref, *rest, rope, b_rows):
    if rope:
        cos_ref, sin_ref = rest[:2]
        rest = rest[2:]
    _, _, ob_ref, oa_ref, og_ref = rest
    tm = x_ref.shape[0]
    r = jnp.dot(x_ref[...], w_ref[...], preferred_element_type=F32)
    for g in range(r.shape[1] // LANES):
        sl = slice(g * LANES, (g + 1) * LANES)
        y = r[:, sl]
        if rope:
            y = y * cos_ref[...] + _swap_halves(y, rope) * sin_ref[...]
        ob_ref[:, sl] = y.astype(BF16)
        if g < A_KV_HEADS:
            oa_ref[pl.ds(g, tm, stride=A_KV_HEADS), :] = y
        else:
            og_ref[pl.ds(b_rows[g - A_KV_HEADS], tm, stride=len(b_rows)), :] = y


def proj_cache(x, w, buf_a, buf_g, layer, *, b_rows, rope=None, tables=None, tm=1024, name='proj_cache'):
    M, K = x.shape
    _, N = w.shape
    tm = min(tm, M)
    nb = M // tm
    ng = len(b_rows)
    in_specs = [pl.BlockSpec((tm, K), lambda m: (m, 0)), pl.BlockSpec((K, N), lambda m: (0, 0))]
    args = [x, w]
    if rope:
        in_specs += [pl.BlockSpec((tm, LANES), lambda m: (m, 0))] * 2
        args += list(tables)
    n_in = len(args)
    in_specs += [pl.BlockSpec(memory_space=pl.ANY)] * 2
    ob, buf_a, buf_g = pl.pallas_call(
        functools.partial(_proj_cache_body, rope=rope, b_rows=tuple(b_rows)),
        grid=(nb,),
        in_specs=in_specs,
        out_specs=[pl.BlockSpec((tm, N), lambda m: (m, 0)),
                   pl.BlockSpec((tm * A_KV_HEADS, LANES), lambda m: (layer * nb + m, 0)),
                   pl.BlockSpec((tm * ng, LANES), lambda m: (layer * nb + m, 0))],
        out_shape=[jax.ShapeDtypeStruct((M, N), BF16),
                   jax.ShapeDtypeStruct(buf_a.shape, F32), jax.ShapeDtypeStruct(buf_g.shape, F32)],
        input_output_aliases={n_in: 1, n_in + 1: 2},
        compiler_params=_params(('arbitrary',)),
        name=name,
    )(*args, buf_a, buf_g)
    return ob, buf_a, buf_g


def _mm_act_body(x_ref, w_ref, o_ref):
    r = jnp.dot(x_ref[...], w_ref[...], preferred_element_type=F32)
    o_ref[...] = jnp.square(jnp.maximum(r, 0.0)).astype(o_ref.dtype)


def mlp_up(x, w, *, tm, tn):
    M, K = x.shape
    _, N = w.shape
    tm, tn = min(tm, M), min(tn, N)
    return pl.pallas_call(
        _mm_act_body,
        grid=(N // tn, M // tm),
        in_specs=[pl.BlockSpec((tm, K), lambda n, m: (m, 0)), pl.BlockSpec((K, tn), lambda n, m: (0, n))],
        out_specs=pl.BlockSpec((tm, tn), lambda n, m: (m, n)),
        out_shape=jax.ShapeDtypeStruct((M, N), BF16),
        compiler_params=_params(('parallel', 'parallel')),
        name='mlp_up',
    )(x, w)


def _mm_ln_body(x_ref, w_ref, r_ref, g_ref, b_ref, o_ref, ob_ref, acc_ref, *, nk, alpha):
    k = pl.program_id(1)

    @pl.when(k == 0)
    def _():
        acc_ref[...] = jnp.zeros_like(acc_ref)

    acc_ref[...] += jnp.dot(x_ref[...], w_ref[...], preferred_element_type=F32)

    @pl.when(k == nk - 1)
    def _():
        y = alpha * r_ref[...] + acc_ref[...]
        mu = jnp.mean(y, axis=-1, keepdims=True)
        yc = y - mu
        var = jnp.mean(yc * yc, axis=-1, keepdims=True)
        out = yc * lax.rsqrt(var + LN_EPS) * g_ref[...] + b_ref[...]
        o_ref[...] = out
        ob_ref[...] = out.astype(BF16)


def matmul_ln(x, w, resid, g, b, *, alpha, tm, tk, name='mm_ln'):
    M, K = x.shape
    _, N = w.shape
    tm, tk = min(tm, M), min(tk, K)
    nk = K // tk
    return pl.pallas_call(
        functools.partial(_mm_ln_body, nk=nk, alpha=alpha),
        grid=(M // tm, nk),
        in_specs=[pl.BlockSpec((tm, tk), lambda m, k: (m, k)),
                  pl.BlockSpec((tk, N), lambda m, k: (k, 0)),
                  pl.BlockSpec((tm, N), lambda m, k: (m, 0)),
                  pl.BlockSpec((1, N), lambda m, k: (0, 0)),
                  pl.BlockSpec((1, N), lambda m, k: (0, 0))],
        out_specs=[pl.BlockSpec((tm, N), lambda m, k: (m, 0)),
                   pl.BlockSpec((tm, N), lambda m, k: (m, 0))],
        out_shape=[jax.ShapeDtypeStruct((M, N), F32), jax.ShapeDtypeStruct((M, N), BF16)],
        scratch_shapes=[pltpu.VMEM((tm, N), F32)],
        compiler_params=_params(('parallel', 'arbitrary')),
        name=name,
    )(x, w, resid, g.reshape(1, N), b.reshape(1, N))


def _branch_body(x_ref, a_ref, b_ref, c_ref, wb_ref, wga_ref, wgb_ref, wgc_ref, o_ref):
    x = x_ref[...]
    acc = None
    for n, (br, wg) in enumerate(((a_ref, wga_ref), (b_ref, wgb_ref), (c_ref, wgc_ref))):
        gate = jax.nn.sigmoid(jnp.dot(x, wg[...], preferred_element_type=F32))
        t = gate * jnp.dot(br[...], wb_ref[n], preferred_element_type=F32)
        acc = t if acc is None else acc + t
    o_ref[...] = acc.astype(o_ref.dtype)


def branch_merge(x16, brA, brB, brC, w_branch, w_gate, *, tm, tn):
    M = x16.shape[0]
    tm = min(tm, M)
    nb = D_MODEL // tn
    gspecs = [pl.BlockSpec((D_MODEL, tn), functools.partial(lambda m, n, base: (0, base + n), base=i * nb))
              for i in range(3)]
    return pl.pallas_call(
        _branch_body,
        grid=(M // tm, nb),
        in_specs=[pl.BlockSpec((tm, D_MODEL), lambda m, n: (m, 0))]
        + [pl.BlockSpec((tm, BRANCH_WIDTH), lambda m, n: (m, 0))] * 3
        + [pl.BlockSpec((3, BRANCH_WIDTH, tn), lambda m, n: (0, 0, n))] + gspecs,
        out_specs=pl.BlockSpec((tm, tn), lambda m, n: (m, n)),
        out_shape=jax.ShapeDtypeStruct((M, D_MODEL), BF16),
        compiler_params=_params(('parallel', 'parallel')),
        name='branch_merge',
    )(x16, brA, brB, brC, w_branch, w_gate, w_gate, w_gate)


def _dsa_prompt_body(qi_ref, misc_ref, kz_ref, qa_ref, ka_ref, va_ref, o_ref, keys_ref, bias_ref,
                     *, top_k, tk, T, qrows):
    QB = qrows
    qb = pl.program_id(1)
    q0 = qb * QB
    nkt = (q0 + QB - 1) // tk + 1
    qpos = q0 + lax.broadcasted_iota(I32, (QB, tk), 0)
    lane = lax.broadcasted_iota(I32, (QB, tk), 1)
    qpos_c = q0 + lax.broadcasted_iota(I32, (QB, LANES), 0)
    lane_c = lax.broadcasted_iota(I32, (QB, LANES), 1)
    wi = misc_ref[:, MISC_WI:MISC_WI + IDX_HEADS]

    def score_tile(j, carry):
        off = pl.multiple_of(j * tk, tk)
        kz = (kz_ref[pl.ds(off, tk), :LANES], kz_ref[pl.ds(off, tk), LANES:])
        acc = jnp.zeros((QB, tk), F32)
        for h in range(IDX_HEADS):
            qp = qi_ref[:, (h // 2) * LANES:(h // 2 + 1) * LANES]
            s = lax.dot_general(qp, kz[h % 2], NT_DIMS, preferred_element_type=F32)
            acc = acc + wi[:, h:h + 1] * jnp.maximum(s, 0.0)
        key = jnp.where(off + lane <= qpos, _sortable(acc), INT_MIN)
        keys_ref[:, pl.ds(off, tk)] = key
        return carry

    lax.fori_loop(0, nkt, score_tile, 0)

    def count(pred):
        def body(j, acc):
            off = pl.multiple_of(j * tk, tk)
            for c in range(tk // LANES):
                kc = keys_ref[:, pl.ds(off + c * LANES, LANES)]
                acc = acc + jnp.where(pred(kc, off + c * LANES + lane_c), 1.0, 0.0)
            return acc
        acc = lax.fori_loop(0, nkt, body, jnp.zeros((QB, LANES), F32))
        return jnp.sum(acc, axis=1, keepdims=True)

    def bcast(v):
        return jnp.broadcast_to(v, (QB, LANES))

    kf = float(top_k)
    c0 = count(lambda kc, kp: kc >= 0)
    lo = jnp.where(c0 >= kf, jnp.int32(0), jnp.int32(INT_MIN))

    def bit_body(i, lo):
        cand = lo + (jnp.int32(1) << (30 - i))
        cand_b = bcast(cand)
        c = count(lambda kc, kp: kc >= cand_b)
        return jnp.where(c >= kf, cand, lo)

    thr = lax.fori_loop(0, 31, bit_body, lo)
    thr_b = bcast(thr)

    need = kf - count(lambda kc, kp: kc > thr_b)
    ceq = count(lambda kc, kp: kc == thr_b)
    nbits = max(T.bit_length(), 1)

    def tie_fn():
        def jb(i, j):
            cand = j + (jnp.int32(1) << (nbits - 1 - i))
            cand_b = bcast(cand)
            g = count(lambda kc, kp: (kc == thr_b) & (kp < cand_b))
            return jnp.where(g <= need, cand, j)
        return lax.fori_loop(0, nbits, jb, jnp.zeros((QB, 1), I32))

    jstar = lax.cond(jnp.max(ceq - need) > 0.0, tie_fn, lambda: jnp.full((QB, 1), 2 ** 30, I32))
    jstar_b = bcast(jstar)

    def bias_tile(j, carry):
        off = pl.multiple_of(j * tk, tk)
        for c in range(tk // LANES):
            kc = keys_ref[:, pl.ds(off + c * LANES, LANES)]
            kp = off + c * LANES + lane_c
            sel = ((kc > thr_b) | ((kc == thr_b) & (kp < jstar_b))) & (kp <= qpos_c)
            bias_ref[:, pl.ds(off + c * LANES, LANES)] = jnp.where(sel, 0.0, NEG)
        return carry

    lax.fori_loop(0, nkt, bias_tile, 0)

    AB = Q_BLOCK
    for sb in range(QB // AB):
        rs = slice(sb * AB, (sb + 1) * AB)
        nkt_s = (q0 + (sb + 1) * AB - 1) // tk + 1
        qgs = [jnp.concatenate([qa_ref[rs, (A_REP * g + r) * HEAD_DIM:(A_REP * g + r + 1) * HEAD_DIM]
                                for r in range(A_REP)], axis=0) for g in range(A_KV_HEADS)]

        def att_tile(j, carry, rs=rs, qgs=qgs):
            off = pl.multiple_of(j * tk, tk)
            b = bias_ref[rs, pl.ds(off, tk)]
            new = []
            for g in range(A_KV_HEADS):
                m, l, acc = carry[g]
                gs = slice(g * HEAD_DIM, (g + 1) * HEAD_DIM)
                kt = ka_ref[pl.ds(off, tk), gs]
                vt = va_ref[pl.ds(off, tk), gs]
                s = lax.dot_general(qgs[g], kt, NT_DIMS, preferred_element_type=F32)
                s = (s.reshape(A_REP, AB, tk) + b[None]).reshape(A_REP * AB, tk)
                m_new = jnp.maximum(m, jnp.max(s, axis=1, keepdims=True))
                corr = jnp.exp(m - m_new)
                p = jnp.exp(s - m_new)
                l = l * corr + jnp.sum(p, axis=1, keepdims=True)
                acc = acc * corr + jnp.dot(p.astype(BF16), vt, preferred_element_type=F32)
                new.append((m_new, l, acc))
            return tuple(new)

        one = (jnp.full((A_REP * AB, 1), NEG, F32), jnp.zeros((A_REP * AB, 1), F32),
               jnp.zeros((A_REP * AB, HEAD_DIM), F32))
        carry = lax.fori_loop(0, nkt_s // 2, lambda j2, c, f=att_tile: f(2 * j2 + 1, f(2 * j2, c)),
                              (one,) * A_KV_HEADS)
        carry = lax.fori_loop(2 * (nkt_s // 2), nkt_s, att_tile, carry)
        for g in range(A_KV_HEADS):
            _, l, acc = carry[g]
            out = acc / l
            for r in range(A_REP):
                h = A_REP * g + r
                o_ref[rs, h * HEAD_DIM:(h + 1) * HEAD_DIM] = out[r * AB:(r + 1) * AB].astype(o_ref.dtype)


def dsa_prompt(qi, misc, kz, q128, k128, v16):
    B, T, _ = qi.shape
    top_k = min(TOPK_MAX, T // 4)
    tk = min(512, T)
    qrows = min(Q_BLOCK, T)
    nb = T // qrows
    kvw = A_KV_HEADS * HEAD_DIM
    return pl.pallas_call(
        functools.partial(_dsa_prompt_body, top_k=top_k, tk=tk, T=T, qrows=qrows),
        grid=(B, nb),
        in_specs=[pl.BlockSpec((None, qrows, IDX_HEADS * IDX_DIM), lambda b, q: (b, q, 0)),
                  pl.BlockSpec((None, qrows, LANES), lambda b, q: (b, q, 0)),
                  pl.BlockSpec((None, T, 2 * LANES), lambda b, q: (b, 0, 0)),
                  pl.BlockSpec((None, qrows, A_HEADS * HEAD_DIM), lambda b, q: (b, q, 0)),
                  pl.BlockSpec((None, T, kvw), lambda b, q: (b, 0, 0)),
                  pl.BlockSpec((None, T, kvw), lambda b, q: (b, 0, 0))],
        out_specs=pl.BlockSpec((None, qrows, A_HEADS * HEAD_DIM), lambda b, q: (b, q, 0)),
        out_shape=jax.ShapeDtypeStruct((B, T, A_HEADS * HEAD_DIM), BF16),
        scratch_shapes=[pltpu.VMEM((qrows, T), I32), pltpu.VMEM((qrows, T), F32)],
        compiler_params=_params(('parallel', 'arbitrary')),
        name='dsa_prompt',
    )(qi, misc, kz, q128, k128, v16)


def _lam(lq1, lk1, lq2, lk2, lam_init):
    return (jnp.exp(jnp.sum(lq1[...] * lk1[...], axis=-1, keepdims=True))
            - jnp.exp(jnp.sum(lq2[...] * lk2[...], axis=-1, keepdims=True)) + lam_init)


def _diff_finish(o0, o1, lam, subln, lam_init):
    a = o0 - lam * o1
    a = a * lax.rsqrt(jnp.mean(a * a, axis=-1, keepdims=True) + RMS_EPS) * subln
    return a * (1.0 - lam_init)


def _diff_prompt_body(lq1, lk1, lq2, lk2, sub_ref, q_ref, k_ref, v_ref, o_ref, *, tq, tk, lam_init):
    qi = pl.program_id(2)
    q0 = qi * tq
    n_full = q0 // tk
    n_all = (q0 + tq - 1) // tk + 1
    qpos = q0 + lax.broadcasted_iota(I32, (tq, tk), 0)
    lane = lax.broadcasted_iota(I32, (tq, tk), 1)
    qs = [q_ref[:, mp * HEAD_DIM:(mp + 1) * HEAD_DIM] for mp in range(2)]

    def tile(j, carry, masked):
        off = pl.multiple_of(j * tk, tk)
        vt = v_ref[pl.ds(off, tk), :]
        new = []
        for mp in range(2):
            m, l, acc = carry[mp]
            kt = k_ref[pl.ds(off, tk), mp * HEAD_DIM:(mp + 1) * HEAD_DIM]
            s = lax.dot_general(qs[mp], kt, NT_DIMS, preferred_element_type=F32)
            if masked:
                s = jnp.where(off + lane <= qpos, s, NEG)
            m_new = jnp.maximum(m, jnp.max(s, axis=1, keepdims=True))
            corr = jnp.exp(m - m_new)
            p = jnp.exp(s - m_new)
            l = l * corr + jnp.sum(p, axis=1, keepdims=True)
            acc = acc * corr + jnp.dot(p.astype(BF16), vt, preferred_element_type=F32)
            new.append((m_new, l, acc))
        return tuple(new)

    one = (jnp.full((tq, 1), NEG, F32), jnp.zeros((tq, 1), F32), jnp.zeros((tq, B_V_DIM), F32))
    def tile2(j2, carry):
        return tile(2 * j2 + 1, tile(2 * j2, carry, masked=False), masked=False)

    carry = lax.fori_loop(0, n_full // 2, tile2, (one, one))
    carry = lax.fori_loop(2 * (n_full // 2), n_full, functools.partial(tile, masked=False), carry)
    carry = lax.fori_loop(n_full, n_all, functools.partial(tile, masked=True), carry)
    outs = [acc / l for (_, l, acc) in carry]
    lam = _lam(lq1, lk1, lq2, lk2, lam_init)
    o_ref[...] = _diff_finish(outs[0], outs[1], lam, sub_ref[...], lam_init).astype(o_ref.dtype)


def diff_prompt(q128, k128, v16, lam_q1, lam_k1, lam_q2, lam_k2, subln, lam_init):
    B, T, _ = q128.shape
    H = B_HEADS
    tq = min(512, T)
    tk = min(512, T)
    pw = 2 * HEAD_DIM
    qb0 = (A_HEADS * HEAD_DIM) // pw
    kb0 = (A_KV_HEADS * HEAD_DIM) // pw
    vec = pl.BlockSpec((1, HEAD_DIM), lambda b, h, q: (0, 0))
    return pl.pallas_call(
        functools.partial(_diff_prompt_body, tq=tq, tk=tk, lam_init=lam_init),
        grid=(B, H, T // tq),
        in_specs=[vec, vec, vec, vec,
                  pl.BlockSpec((1, B_V_DIM), lambda b, h, q: (0, 0)),
                  pl.BlockSpec((None, tq, pw), lambda b, h, q: (b, q, qb0 + h)),
                  pl.BlockSpec((None, T, pw), lambda b, h, q: (b, 0, kb0 + h)),
                  pl.BlockSpec((None, T, B_V_DIM), lambda b, h, q: (b, 0, kb0 + h))],
        out_specs=pl.BlockSpec((None, tq, B_V_DIM), lambda b, h, q: (b, q, h)),
        out_shape=jax.ShapeDtypeStruct((B, T, H * B_V_DIM), BF16),
        compiler_params=_params(('parallel', 'parallel', 'arbitrary')),
        name='diff_prompt',
    )(lam_q1.reshape(1, -1), lam_k1.reshape(1, -1), lam_q2.reshape(1, -1), lam_k2.reshape(1, -1),
      subln.reshape(1, -1), q128, k128, v16)


def _gla_body(q_ref, k_ref, v_ref, rc_ref, misc_ref, w2_ref, gb_ref, gn_ref, s0_ref, o_ref, sfin_ref, st_ref,
              *, tb, chunk, t_valid, nt):
    t = pl.program_id(1)

    @pl.when(t == 0)
    def _():
        st_ref[...] = s0_ref[...]

    x = jnp.dot(misc_ref[...].astype(BF16), w2_ref[...], preferred_element_type=F32) + gb_ref[...]
    la = (jnp.minimum(x, 0.0) - jnp.log(1.0 + jnp.exp(-jnp.abs(x)))) * (1.0 / GLA_TAU)
    W = C_HEADS * C_K_DIM
    row = lax.broadcasted_iota(I32, (tb, W), 0)
    if t_valid is not None:
        la = jnp.where(t * tb + row < t_valid, la, 0.0)
    rowc = row % chunk
    b = la
    sh = 1
    while sh < chunk:
        b = b + jnp.where(rowc >= sh, pltpu.roll(b, sh, axis=0), 0.0)
        sh *= 2
    k = k_ref[...]
    qe = (q_ref[...] * (C_K_DIM ** -0.5) * jnp.exp(b)).astype(BF16)
    ke = (k * jnp.exp(-b)).astype(BF16)
    v16 = v_ref[...].astype(BF16)
    rc = rc_ref[...]
    gn = gn_ref[...]
    tril = lax.broadcasted_iota(I32, (chunk, chunk), 0) >= lax.broadcasted_iota(I32, (chunk, chunk), 1)
    for c in range(tb // chunk):
        r0 = c * chunk
        bl = b[r0 + chunk - 1:r0 + chunk, :]
        kd = (k[r0:r0 + chunk] * jnp.exp(bl - b[r0:r0 + chunk])).astype(BF16)
        dec = jnp.exp(bl)
        for h in range(C_HEADS):
            ck = slice(h * C_K_DIM, (h + 1) * C_K_DIM)
            cv = slice(h * C_V_DIM, (h + 1) * C_V_DIM)
            qe_c = qe[r0:r0 + chunk, ck]
            v_c = v16[r0:r0 + chunk, cv]
            att = lax.dot_general(qe_c, ke[r0:r0 + chunk, ck], NT_DIMS, preferred_element_type=F32)
            att = jnp.where(tril, att, 0.0).astype(BF16)
            st = st_ref[h]
            o_c = (lax.dot_general(qe_c, st.astype(BF16), NT_DIMS, preferred_element_type=F32)
                   + jnp.dot(att, v_c, preferred_element_type=F32))
            st_ref[h] = st * dec[:, ck] + lax.dot_general(v_c, kd[:, ck], TN_DIMS, preferred_element_type=F32)
            o_n = o_c * lax.rsqrt(jnp.mean(o_c * o_c, axis=-1, keepdims=True) + RMS_EPS) * gn
            r_c = rc[r0:r0 + chunk, cv]
            o_ref[r0:r0 + chunk, cv] = (o_n * (r_c * jax.nn.sigmoid(r_c))).astype(o_ref.dtype)

    @pl.when(t == nt - 1)
    def _():
        sfin_ref[...] = st_ref[...]


def gla(cproj, misc, w2pad, gb, gn, s0t, *, t_valid=None):
    B, T, _ = cproj.shape
    chunk = GLA_CHUNK
    tb = min(256, T)
    nt = T // tb
    W = C_HEADS * C_K_DIM
    V = C_HEADS * C_V_DIM
    st_spec = pl.BlockSpec((None, C_HEADS, C_V_DIM, C_K_DIM), lambda b, t: (b, 0, 0, 0))
    return pl.pallas_call(
        functools.partial(_gla_body, tb=tb, chunk=chunk, t_valid=t_valid, nt=nt),
        grid=(B, nt),
        in_specs=[pl.BlockSpec((None, tb, W), lambda b, t: (b, t, 0)),
                  pl.BlockSpec((None, tb, W), lambda b, t: (b, t, 1)),
                  pl.BlockSpec((None, tb, V), lambda b, t: (b, t, 1)),
                  pl.BlockSpec((None, tb, V), lambda b, t: (b, t, 2)),
                  pl.BlockSpec((None, tb, LANES), lambda b, t: (b, t, 0)),
                  pl.BlockSpec((LANES, W), lambda b, t: (0, 0)),
                  pl.BlockSpec((1, W), lambda b, t: (0, 0)),
                  pl.BlockSpec((1, C_V_DIM), lambda b, t: (0, 0)),
                  st_spec],
        out_specs=[pl.BlockSpec((None, tb, V), lambda b, t: (b, t, 0)), st_spec],
        out_shape=[jax.ShapeDtypeStruct((B, T, V), BF16),
                   jax.ShapeDtypeStruct((B, C_HEADS, C_V_DIM, C_K_DIM), F32)],
        scratch_shapes=[pltpu.VMEM((C_HEADS, C_V_DIM, C_K_DIM), F32)],
        compiler_params=_params(('parallel', 'arbitrary')),
        name='gla',
    )(cproj, cproj, cproj, cproj, misc, w2pad, gb.reshape(1, W), gn.reshape(1, C_V_DIM), s0t)


def _page_specs(block, layer, pps):
    def mk(i):
        return pl.BlockSpec((None, None) + block,
                            lambda b, p, pt: (layer, pt[b, p * pps + i]) + (0,) * len(block))
    return [mk(i) for i in range(pps)]


def _dsa_sample_score_body(pt_ref, qi_ref, wi_ref, kin_ref, *rest, pps, n_pages):
    pages, o_ref = rest[:pps], rest[pps]
    p = pl.program_id(1)
    qi = qi_ref[...]
    wi = wi_ref[...]

    @pl.when(p == 0)
    def _():
        o_ref[...] = jnp.full(o_ref.shape, -jnp.inf, F32)
        s = jnp.sum(qi.astype(F32) * kin_ref[...], axis=1, keepdims=True)
        snew = jnp.sum(wi * jnp.maximum(s, 0.0), axis=0, keepdims=True)
        lane = lax.broadcasted_iota(I32, (1, PAGE_SIZE), 1)
        o_ref[n_pages:n_pages + 1, :] = jnp.where(lane == 0, snew, -jnp.inf)

    kpt = jnp.concatenate([pg[...].astype(BF16) for pg in pages], axis=1)
    s = jnp.dot(qi, kpt, preferred_element_type=F32)
    sc = jnp.sum(wi * jnp.maximum(s, 0.0), axis=0, keepdims=True)
    for i in range(pps):
        o_ref[pl.ds(p * pps + i, 1), :] = sc[:, i * PAGE_SIZE:(i + 1) * PAGE_SIZE]


def _topk_bias(sc, top_k, n_valid):
    R = sc.shape[0]
    key = _sortable(sc)
    pos = lax.broadcasted_iota(I32, sc.shape, 0) * PAGE_SIZE + lax.broadcasted_iota(I32, sc.shape, 1)
    key = jnp.where(pos < n_valid, key, INT_MIN)

    def count(hit):
        c = jnp.sum(jnp.where(hit, 1.0, 0.0), axis=1, keepdims=True)
        return jnp.sum(c, axis=0, keepdims=True)

    kf = float(top_k)
    lo = jnp.where(count(key >= 0) >= kf, jnp.int32(0), jnp.int32(INT_MIN))

    def bit_body(i, lo):
        cand = lo + (jnp.int32(1) << (30 - i))
        return jnp.where(count(key >= cand) >= kf, cand, lo)

    thr = lax.fori_loop(0, 31, bit_body, lo)
    need = kf - count(key > thr)
    nbits = (R * PAGE_SIZE).bit_length()

    def jb(i, j):
        cand = j + (jnp.int32(1) << (nbits - 1 - i))
        return jnp.where(count((key == thr) & (pos < cand)) <= need, cand, j)

    jstar = lax.fori_loop(0, nbits, jb, jnp.zeros((1, 1), I32))
    sel = ((key > thr) | ((key == thr) & (pos < jstar))) & (pos < n_valid)
    return jnp.where(sel, 0.0, NEG)


def _dsa_sample_attn_body(pt_ref, sc_ref, q_ref, kn_ref, vn_ref, *rest, pps, n_pages, top_k):
    kpages, vpages = rest[:pps], rest[pps:2 * pps]
    o_ref, bias_ref, m_ref, l_ref, acc_ref = rest[2 * pps:]
    p = pl.program_id(1)
    q = q_ref[...]
    first = lax.broadcasted_iota(I32, (A_HEADS, PAGE_SIZE), 0) < A_REP

    @pl.when(p == 0)
    def _():
        bias_ref[...] = _topk_bias(sc_ref[...], top_k, n_pages * PAGE_SIZE + 1)
        m_ref[...] = jnp.full(m_ref.shape, NEG, F32)
        l_ref[...] = jnp.zeros_like(l_ref)
        acc_ref[...] = jnp.zeros_like(acc_ref)

    def kv(refs, g):
        return jnp.concatenate([r[pl.ds(g, PAGE_SIZE, stride=A_KV_HEADS), :].astype(BF16) for r in refs], axis=0)

    W = pps * PAGE_SIZE
    first_w = lax.broadcasted_iota(I32, (A_HEADS, W), 0) < A_REP
    s0 = lax.dot_general(q, kv(kpages, 0), NT_DIMS, preferred_element_type=F32)
    s1 = lax.dot_general(q, kv(kpages, 1), NT_DIMS, preferred_element_type=F32)
    bias = jnp.concatenate([bias_ref[pl.ds(p * pps + i, 1), :] for i in range(pps)], axis=1)
    s = jnp.where(first_w, s0, s1) + bias
    m = m_ref[...]
    m_new = jnp.maximum(m, jnp.max(s, axis=1, keepdims=True))
    corr = jnp.exp(m - m_new)
    pr = jnp.exp(s - m_new)
    l_ref[...] = l_ref[...] * corr + jnp.sum(pr, axis=1, keepdims=True)
    pb = pr.astype(BF16)
    pv = jnp.where(first, jnp.dot(pb, kv(vpages, 0), preferred_element_type=F32),
                   jnp.dot(pb, kv(vpages, 1), preferred_element_type=F32))
    acc_ref[...] = acc_ref[...] * corr + pv
    m_ref[...] = m_new

    @pl.when(p == pl.num_programs(1) - 1)
    def _():
        s = (jnp.sum(q.astype(F32) * kn_ref[...], axis=1, keepdims=True)
             + bias_ref[n_pages:n_pages + 1, 0:1])
        m = m_ref[...]
        m_new = jnp.maximum(m, s)
        corr = jnp.exp(m - m_new)
        pr = jnp.exp(s - m_new)
        l = l_ref[...] * corr + pr
        o_ref[...] = (acc_ref[...] * corr + pr * vn_ref[...]) / l


def dsa_sample(layer, page_table, qi, wi, ki_new, qa, ka_new, va_new, cache_idx_kt, cache_a_k, cache_a_v):
    DB, n_pages = page_table.shape
    pps_s = math.gcd(32, n_pages)
    pps = math.gcd(16, n_pages)
    R = ((n_pages + 1 + 7) // 8) * 8
    L = n_pages * PAGE_SIZE + 1
    top_k = min(TOPK_MAX, L // 4)
    per_b = lambda *blk: pl.BlockSpec((None,) + blk, lambda b, p, pt: (b,) + (0,) * len(blk))
    scores = pl.pallas_call(
        functools.partial(_dsa_sample_score_body, pps=pps_s, n_pages=n_pages),
        grid_spec=pltpu.PrefetchScalarGridSpec(
            num_scalar_prefetch=1, grid=(DB, n_pages // pps_s),
            in_specs=[per_b(IDX_HEADS, IDX_DIM), per_b(IDX_HEADS, 1), per_b(1, IDX_DIM)]
            + _page_specs((IDX_DIM, PAGE_SIZE), layer, pps_s),
            out_specs=per_b(R, PAGE_SIZE)),
        out_shape=jax.ShapeDtypeStruct((DB, R, PAGE_SIZE), F32),
        compiler_params=_params(('parallel', 'arbitrary')),
        name='dsa_sample_scores',
    )(page_table, qi, wi, ki_new, *([cache_idx_kt] * pps_s))
    rows = PAGE_SIZE * A_KV_HEADS
    return pl.pallas_call(
        functools.partial(_dsa_sample_attn_body, pps=pps, n_pages=n_pages, top_k=top_k),
        grid_spec=pltpu.PrefetchScalarGridSpec(
            num_scalar_prefetch=1, grid=(DB, n_pages // pps),
            in_specs=[per_b(R, PAGE_SIZE), per_b(A_HEADS, HEAD_DIM), per_b(A_HEADS, HEAD_DIM), per_b(A_HEADS, HEAD_DIM)]
            + _page_specs((rows, HEAD_DIM), layer, pps) + _page_specs((rows, HEAD_DIM), layer, pps),
            out_specs=per_b(A_HEADS, HEAD_DIM),
            scratch_shapes=[pltpu.VMEM((R, PAGE_SIZE), F32), pltpu.VMEM((A_HEADS, 1), F32),
                            pltpu.VMEM((A_HEADS, 1), F32), pltpu.VMEM((A_HEADS, HEAD_DIM), F32)]),
        out_shape=jax.ShapeDtypeStruct((DB, A_HEADS, HEAD_DIM), F32),
        compiler_params=_params(('parallel', 'arbitrary')),
        name='dsa_sample_attn',
    )(page_table, scores, qa, ka_new, va_new, *([cache_a_k] * pps), *([cache_a_v] * pps))


def _diff_sample_body(pt_ref, lq1, lk1, lq2, lk2, sub_ref, q_ref, kn_ref, vn_ref, *rest, pps, lam_init):
    kpages, vpages = rest[:pps], rest[pps:2 * pps]
    o_ref, m_ref, l_ref, acc_ref = rest[2 * pps:]
    p = pl.program_id(1)
    NJ = 2 * B_HEADS
    q = q_ref[...]
    rowk = lax.broadcasted_iota(I32, (NJ, PAGE_SIZE), 0)

    @pl.when(p == 0)
    def _():
        m_ref[...] = jnp.full(m_ref.shape, NEG, F32)
        l_ref[...] = jnp.zeros_like(l_ref)
        acc_ref[...] = jnp.zeros_like(acc_ref)

    def rows(refs, j):
        return jnp.concatenate([r[pl.ds(j, PAGE_SIZE, stride=NJ), :].astype(BF16) for r in refs], axis=0)

    W = pps * PAGE_SIZE
    roww = lax.broadcasted_iota(I32, (NJ, W), 0)
    s = jnp.zeros((NJ, W), F32)
    for j in range(NJ):
        sj = lax.dot_general(q, rows(kpages, j), NT_DIMS, preferred_element_type=F32)
        s = jnp.where(roww == j, sj, s)
    m = m_ref[...]
    m_new = jnp.maximum(m, jnp.max(s, axis=1, keepdims=True))
    corr = jnp.exp(m - m_new)
    pr = jnp.exp(s - m_new)
    l_ref[...] = l_ref[...] * corr + jnp.sum(pr, axis=1, keepdims=True)
    pb = pr.astype(BF16)
    halves = []
    for c in range(B_V_DIM // LANES):
        pv = jnp.zeros((NJ, LANES), F32)
        for h in range(B_HEADS):
            ph = jnp.dot(pb, rows(vpages, c * B_HEADS + h), preferred_element_type=F32)
            pv = jnp.where(rowk // 2 == h, ph, pv)
        halves.append(pv)
    acc_ref[...] = acc_ref[...] * corr + jnp.concatenate(halves, axis=1)
    m_ref[...] = m_new

    @pl.when(p == pl.num_programs(1) - 1)
    def _():
        s = jnp.sum(q.astype(F32) * kn_ref[...], axis=1, keepdims=True)
        m = m_ref[...]
        m_new = jnp.maximum(m, s)
        corr = jnp.exp(m - m_new)
        pr = jnp.exp(s - m_new)
        l = l_ref[...] * corr + pr
        o = (acc_ref[...] * corr + pr * vn_ref[...]) / l
        lam = _lam(lq1, lk1, lq2, lk2, lam_init)
        for h in range(B_HEADS):
            o_ref[h:h + 1, :] = _diff_finish(o[2 * h:2 * h + 1], o[2 * h + 1:2 * h + 2], lam, sub_ref[...], lam_init)


def diff_sample(layer, page_table, qb, kb_new, vb_new, cache_b_k, cache_b_v,
                lam_q1, lam_k1, lam_q2, lam_k2, subln, lam_init):
    DB, n_pages = page_table.shape
    pps = math.gcd(8, n_pages)
    NJ = 2 * B_HEADS
    per_b = lambda *blk: pl.BlockSpec((None,) + blk, lambda b, p, pt: (b,) + (0,) * len(blk))
    vec = lambda w: pl.BlockSpec((1, w), lambda b, p, pt: (0, 0))
    return pl.pallas_call(
        functools.partial(_diff_sample_body, pps=pps, lam_init=lam_init),
        grid_spec=pltpu.PrefetchScalarGridSpec(
            num_scalar_prefetch=1, grid=(DB, n_pages // pps),
            in_specs=[vec(HEAD_DIM)] * 4 + [vec(B_V_DIM),
                      per_b(NJ, HEAD_DIM), per_b(NJ, HEAD_DIM), per_b(NJ, B_V_DIM)]
            + _page_specs((PAGE_SIZE * NJ, HEAD_DIM), layer, pps)
            + _page_specs((PAGE_SIZE * NJ, LANES), layer, pps),
            out_specs=per_b(B_HEADS, B_V_DIM),
            scratch_shapes=[pltpu.VMEM((NJ, 1), F32), pltpu.VMEM((NJ, 1), F32), pltpu.VMEM((NJ, B_V_DIM), F32)]),
        out_shape=jax.ShapeDtypeStruct((DB, B_HEADS, B_V_DIM), F32),
        compiler_params=_params(('parallel', 'arbitrary')),
        name='diff_sample',
    )(page_table, lam_q1.reshape(1, -1), lam_k1.reshape(1, -1), lam_q2.reshape(1, -1), lam_k2.reshape(1, -1),
      subln.reshape(1, -1), qb, kb_new, vb_new, *([cache_b_k] * pps), *([cache_b_v] * pps))


def _split_w_in(w):
    src = {}
    off = 0
    for name, width in SRC_SEGMENTS:
        src[name] = w[:, off:off + width]
        off += width
    cat = lambda *names: jnp.concatenate([src[n] for n in names], axis=1).astype(BF16)
    pad = jnp.zeros((w.shape[0], LANES - IDX_DIM - IDX_HEADS - GLA_RANK), w.dtype)
    return {
        'q128': cat('qa', 'qb'), 'k128': cat('ka', 'kb'), 'qi': cat('qi'), 'v': cat('va', 'vb'),
        'c': cat('qc', 'kc', 'vc', 'rc'), 'gate': cat('gate'),
        'misc': jnp.concatenate([src['ki'], src['wi'], src['gc'], pad], axis=1).astype(BF16),
    }


def _rope_tables(pos):
    def tab(dh):
        half = dh // 2
        inv_freq = ROPE_THETA ** (-jnp.arange(half, dtype=F32) / half)
        ang = pos.astype(F32)[:, None] * inv_freq[None, :]
        c, s = jnp.cos(ang), jnp.sin(ang)
        reps = LANES // dh
        return jnp.tile(jnp.concatenate([c, c], axis=1), (1, reps)), jnp.tile(jnp.concatenate([-s, s], axis=1), (1, reps))
    c128, s128 = tab(HEAD_DIM)
    c64, s64 = tab(IDX_DIM)
    n = pos.shape[0]
    tail_c = jnp.concatenate([jnp.full((n, IDX_HEADS), IDX_HEADS ** -0.5, F32),
                              jnp.ones((n, LANES - IDX_DIM - IDX_HEADS), F32)], axis=1)
    cm = jnp.concatenate([c64[:, :IDX_DIM], tail_c], axis=1)
    sm = jnp.concatenate([s64[:, :IDX_DIM], jnp.zeros((n, LANES - IDX_DIM), F32)], axis=1)
    return (c128, s128), (c64, s64), (cm, sm)


BK_ROWS = tuple(range(2 * B_HEADS))
BV_ROWS = tuple((g % 2) * B_HEADS + g // 2 for g in range(2 * B_HEADS))


def _project(x16, wg, tabs, cache=None, layer=0):
    t128, t64, tm_ = tabs
    q128, = proj(x16, wg['q128'], out_dtypes=(BF16,), rope=HEAD_DIM, tables=t128, scale=HEAD_DIM ** -0.5, name='proj_q128')
    qi, = proj(x16, wg['qi'], out_dtypes=(BF16,), rope=IDX_DIM, tables=t64, scale=IDX_DIM ** -0.5, name='proj_qi')
    misc, kz = proj(x16, wg['misc'], out_dtypes=(F32,), rope=IDX_DIM, tables=tm_, emit_kz=True, name='proj_misc')
    cproj, = proj(x16, wg['c'], out_dtypes=(F32,), name='proj_c')
    out = dict(q128=q128, qi=qi, misc=misc, kz=kz, c=cproj)
    if cache is None:
        out['k128f'], out['k128'] = proj(x16, wg['k128'], out_dtypes=(F32, BF16), rope=HEAD_DIM, tables=t128, tn=1280,
                                         name='proj_k128')
        out['vf'], out['v16'] = proj(x16, wg['v'], out_dtypes=(F32, BF16), tn=1280, name='proj_v')
        return out, None
    out['k128'], ak, bk = proj_cache(x16, wg['k128'], cache['a_k'], cache['b_k'], layer, b_rows=BK_ROWS,
                                     rope=HEAD_DIM, tables=t128, name='proj_k128c')
    out['v16'], av, bv = proj_cache(x16, wg['v'], cache['a_v'], cache['b_v'], layer, b_rows=BV_ROWS, name='proj_vc')
    return out, dict(a_k=ak, b_k=bk, a_v=av, b_v=bv)


def _tail(x, x16, oA, oB, oC, lw, alpha):
    g = branch_merge(x16, oA, oB, oC, lw['w_branch'], lw['w_gate'], tm=1024, tn=512)
    x1, x1b = matmul_ln(g, lw['w_out'], x, lw['ln1_g'], lw['ln1_b'], alpha=alpha, tm=512, tk=1024, name='out_ln1')
    hid = mlp_up(x1b, lw['w_up'], tm=1024, tn=1024)
    return matmul_ln(hid, lw['w_down'], x1, lw['ln2_g'], lw['ln2_b'], alpha=alpha, tm=512, tk=1024, name='down_ln2')


def kernel(x_prompt, x_sample, cache_a_k, cache_a_v, cache_idx_k, cache_b_k, cache_b_v, state_gla, page_table,
           w_in, gla_w2, gla_b, lam_q1, lam_k1, lam_q2, lam_k2, diff_subln, gla_norm, w_branch, w_out,
           ln1_g, ln1_b, w_up, w_down, ln2_g, ln2_b):
    B, T, D = x_prompt.shape
    DB, Ts, _ = x_sample.shape
    assert Ts == 1
    DEPTH = w_in.shape[0]
    n_pages = page_table.shape[1]
    P = n_pages * PAGE_SIZE
    alpha = (2.0 * DEPTH) ** 0.25
    n_phys = cache_a_k.shape[1]
    ca_k = cache_a_k.reshape(DEPTH, n_phys, PAGE_SIZE * A_KV_HEADS, HEAD_DIM)
    ca_v = cache_a_v.reshape(DEPTH, n_phys, PAGE_SIZE * A_KV_HEADS, HEAD_DIM)
    cb_k = cache_b_k.reshape(DEPTH, n_phys, PAGE_SIZE * B_HEADS * 2, HEAD_DIM)
    cb_v = jnp.transpose(cache_b_v.reshape(DEPTH, n_phys, PAGE_SIZE, B_HEADS, B_V_DIM // LANES, LANES),
                         (0, 1, 2, 4, 3, 5)).reshape(DEPTH, n_phys, PAGE_SIZE * B_HEADS * 2, LANES)
    ci_kt = jnp.swapaxes(cache_idx_k, 2, 3)
    tabs_p = _rope_tables(jnp.tile(jnp.arange(T), B))
    tabs_s = _rope_tables(jnp.tile(P + jnp.arange(Ts), DB))
    TS_PAD = GLA_CHUNK
    AKV = A_KV_HEADS * HEAD_DIM

    xp = x_prompt.reshape(B * T, D)
    xs = x_sample.reshape(DB * Ts, D)
    xp16, xs16 = xp.astype(BF16), xs.astype(BF16)
    outs = {k: [] for k in ('a_k_s', 'a_v_s', 'i_k_p', 'i_k_s', 'b_k_s', 'b_v_s', 'g_p', 'g_s')}
    cache_p = {'a_k': jnp.zeros((DEPTH * B * T * A_KV_HEADS, LANES), F32),
               'a_v': jnp.zeros((DEPTH * B * T * A_KV_HEADS, LANES), F32),
               'b_k': jnp.zeros((DEPTH * B * T * 2 * B_HEADS, LANES), F32),
               'b_v': jnp.zeros((DEPTH * B * T * 2 * B_HEADS, LANES), F32)}
    for l in range(DEPTH):
        lam_init = 0.8 - 0.6 * math.exp(-0.3 * l)
        wg = _split_w_in(w_in[l])
        lw = {'w_branch': w_branch[l].astype(BF16), 'w_out': w_out[l].astype(BF16), 'w_up': w_up[l].astype(BF16),
              'w_down': w_down[l].astype(BF16), 'w_gate': wg['gate'], 'ln1_g': ln1_g[l], 'ln1_b': ln1_b[l],
              'ln2_g': ln2_g[l], 'ln2_b': ln2_b[l]}
        w2pad = jnp.zeros((LANES, C_HEADS * C_K_DIM), F32).at[MISC_GC:MISC_GC + GLA_RANK].set(gla_w2[l]).astype(BF16)
        lam_args = (lam_q1[l], lam_k1[l], lam_q2[l], lam_k2[l], diff_subln[l], lam_init)

        pr, cache_p = _project(xp16, wg, tabs_p, cache_p, l)
        r3 = lambda a: a.reshape(B, T, a.shape[-1])
        oA = dsa_prompt(r3(pr['qi']), r3(pr['misc']), r3(pr['kz']), r3(pr['q128']), r3(pr['k128']), r3(pr['v16']))
        oB = diff_prompt(r3(pr['q128']), r3(pr['k128']), r3(pr['v16']), *lam_args)
        oC, Sp = gla(r3(pr['c']), r3(pr['misc']), w2pad, gla_b[l], gla_norm[l],
                     jnp.zeros((B, C_HEADS, C_V_DIM, C_K_DIM), F32))
        xp, xp16 = _tail(xp, xp16, oA.reshape(B * T, -1), oB.reshape(B * T, -1), oC.reshape(B * T, -1), lw, alpha)
        outs['i_k_p'].append(pr['misc'][:, :IDX_DIM].reshape(B, T, IDX_DIM))
        outs['g_p'].append(jnp.swapaxes(Sp, -1, -2))

        ps, _ = _project(xs16, wg, tabs_s)
        ka_new = ps['k128f'][:, :AKV].reshape(DB, A_KV_HEADS, HEAD_DIM)
        va_new = ps['vf'][:, :AKV].reshape(DB, A_KV_HEADS, HEAD_DIM)
        kb_new = ps['k128f'][:, AKV:].reshape(DB, 2 * B_HEADS, HEAD_DIM)
        vb_new = ps['vf'][:, AKV:].reshape(DB, B_HEADS, B_V_DIM)
        ki_new = ps['misc'][:, :IDX_DIM]
        oA = dsa_sample(l, page_table,
                        ps['qi'].reshape(DB, IDX_HEADS, IDX_DIM),
                        ps['misc'][:, MISC_WI:MISC_WI + IDX_HEADS].reshape(DB, IDX_HEADS, 1),
                        ki_new.reshape(DB, 1, IDX_DIM),
                        ps['q128'][:, :A_HEADS * HEAD_DIM].reshape(DB, A_HEADS, HEAD_DIM),
                        jnp.repeat(ka_new, A_REP, axis=1), jnp.repeat(va_new, A_REP, axis=1),
                        ci_kt, ca_k, ca_v)
        oB = diff_sample(l, page_table, ps['q128'][:, A_HEADS * HEAD_DIM:].reshape(DB, 2 * B_HEADS, HEAD_DIM),
                         kb_new, jnp.repeat(vb_new, 2, axis=1), cb_k, cb_v, *lam_args)
        padt = lambda a: jnp.pad(a.reshape(DB, Ts, -1), ((0, 0), (0, TS_PAD - Ts), (0, 0)))
        oC, Ss = gla(padt(ps['c']), padt(ps['misc']), w2pad, gla_b[l], gla_norm[l],
                     jnp.swapaxes(state_gla[l], -1, -2), t_valid=Ts)
        xs, xs16 = _tail(xs, xs16, oA.reshape(DB, -1).astype(BF16), oB.reshape(DB, -1).astype(BF16),
                         oC[:, :Ts].reshape(DB * Ts, -1), lw, alpha)
        outs['a_k_s'].append(ka_new.reshape(DB, Ts, A_KV_HEADS, HEAD_DIM))
        outs['a_v_s'].append(va_new.reshape(DB, Ts, A_KV_HEADS, HEAD_DIM))
        outs['i_k_s'].append(ki_new.reshape(DB, Ts, IDX_DIM))
        outs['b_k_s'].append(kb_new.reshape(DB, Ts, B_HEADS, 2, HEAD_DIM))
        outs['b_v_s'].append(vb_new.reshape(DB, Ts, B_HEADS, B_V_DIM))
        outs['g_s'].append(jnp.swapaxes(Ss, -1, -2))

    st = {k: jnp.stack(v) for k, v in outs.items()}
    st['a_k_p'] = cache_p['a_k'].reshape(DEPTH, B, T, A_KV_HEADS, HEAD_DIM)
    st['a_v_p'] = cache_p['a_v'].reshape(DEPTH, B, T, A_KV_HEADS, HEAD_DIM)
    st['b_k_p'] = cache_p['b_k'].reshape(DEPTH, B, T, B_HEADS, 2, HEAD_DIM)
    st['b_v_p'] = jnp.transpose(cache_p['b_v'].reshape(DEPTH, B, T, B_V_DIM // LANES, B_HEADS, LANES),
                                (0, 1, 2, 4, 3, 5)).reshape(DEPTH, B, T, B_HEADS, B_V_DIM)
    return (xp.reshape(B, T, D), xs.reshape(DB, Ts, D),
            st['a_k_p'], st['a_k_s'], st['a_v_p'], st['a_v_s'], st['i_k_p'], st['i_k_s'],
            st['b_k_p'], st['b_k_s'], st['b_v_p'], st['b_v_s'], st['g_p'], st['g_s'])
```

```python
import functools
import math

import jax
import jax.numpy as jnp
from jax import lax
from jax.experimental import pallas as pl
from jax.experimental.pallas import tpu as pltpu

F32 = jnp.float32
BF16 = jnp.bfloat16
I32 = jnp.int32
I16 = jnp.int16

LANES = 128
HEAD_DIM = 128
A_HEADS = 8
A_KV_HEADS = 2
A_REP = A_HEADS // A_KV_HEADS
IDX_HEADS = 16
IDX_DIM = 64
TOPK_MAX = 256
B_HEADS = 4
B_V_DIM = 256
C_HEADS = 4
C_V_DIM = 256
C_K_DIM = 128
GLA_RANK = 16
GLA_TAU = 16.0
GLA_CHUNK = 32
PAGE_SIZE = 128
Q_BLOCK = 128
ROPE_THETA = 10000.0
LN_EPS = 1e-5
RMS_EPS = 1e-6
BRANCH_WIDTH = 1024
D_MODEL = 2048

NEG = -1e30
INT_MIN = -(2 ** 31)
VMEM_LIMIT = 56 * 1024 * 1024

NT_DIMS = (((1,), (1,)), ((), ()))
TN_DIMS = (((0,), (0,)), ((), ()))

MISC_WI = IDX_DIM
MISC_GC = IDX_DIM + IDX_HEADS

SRC_SEGMENTS = (
    ('qa', 1024), ('ka', 256), ('va', 256), ('qi', 1024), ('ki', 64), ('wi', 16),
    ('qb', 1024), ('kb', 1024), ('vb', 1024), ('qc', 512), ('kc', 512), ('vc', 1024),
    ('gc', 16), ('rc', 1024), ('gate', 6144),
)


def _params(sem):
    return pltpu.CompilerParams(dimension_semantics=sem, vmem_limit_bytes=VMEM_LIMIT)


def _sortable(x):
    bits = lax.bitcast_convert_type(x, I32)
    return jnp.where(bits < 0, bits ^ jnp.int32(0x7FFFFFFF), bits)


def _swap_halves(blk, dh):
    if dh == LANES:
        return pltpu.roll(blk, LANES // 2, axis=1)
    lane = lax.broadcasted_iota(I32, blk.shape, 1)
    half = dh // 2
    return jnp.where(lane % dh < half, pltpu.roll(blk, LANES - half, axis=1), pltpu.roll(blk, half, axis=1))


def _proj_body(x_ref, w_ref, *rest, rope, scale, emit_kz):
    if rope:
        cos_ref, sin_ref = rest[:2]
        outs = rest[2:]
    else:
        outs = rest
    r = jnp.dot(x_ref[...], w_ref[...], preferred_element_type=F32)
    tn = r.shape[1]
    if emit_kz:
        o_ref, kz_ref = outs
        y = r * cos_ref[...] + _swap_halves(r, rope) * sin_ref[...]
        o_ref[...] = y
        lane = lax.broadcasted_iota(I32, y.shape, 1)
        kz0 = jnp.where(lane < IDX_DIM, y, 0.0)
        kz_ref[:, :LANES] = kz0.astype(BF16)
        kz_ref[:, LANES:] = pltpu.roll(kz0, IDX_DIM, axis=1).astype(BF16)
        return
    if rope:
        cos = cos_ref[...]
        sin = sin_ref[...]
        for g in range(tn // LANES):
            sl = slice(g * LANES, (g + 1) * LANES)
            blk = r[:, sl]
            y = blk * cos + _swap_halves(blk, rope) * sin
            if scale != 1.0:
                y = y * scale
            for o in outs:
                o[:, sl] = y.astype(o.dtype)
        return
    if scale != 1.0:
        r = r * scale
    for o in outs:
        o[...] = r.astype(o.dtype)


def proj(x, w, *, out_dtypes, rope=None, tables=None, scale=1.0, emit_kz=False, tm=1024, tn=1024, name='proj'):
    M, K = x.shape
    _, N = w.shape
    tm, tn = min(tm, M), min(tn, N)
    assert M % tm == 0 and N % tn == 0
    in_specs = [pl.BlockSpec((tm, K), lambda n, m: (m, 0)), pl.BlockSpec((K, tn), lambda n, m: (0, n))]
    args = [x, w]
    if rope:
        in_specs += [pl.BlockSpec((tm, LANES), lambda n, m: (m, 0))] * 2
        args += list(tables)
    out_specs = [pl.BlockSpec((tm, tn), lambda n, m: (m, n)) for _ in out_dtypes]
    out_shape = [jax.ShapeDtypeStruct((M, N), dt) for dt in out_dtypes]
    if emit_kz:
        out_specs.append(pl.BlockSpec((tm, 2 * LANES), lambda n, m: (m, 0)))
        out_shape.append(jax.ShapeDtypeStruct((M, 2 * LANES), BF16))
    return pl.pallas_call(
        functools.partial(_proj_body, rope=rope, scale=scale, emit_kz=emit_kz),
        grid=(N // tn, M // tm),
        in_specs=in_specs, out_specs=out_specs, out_shape=out_shape,
        compiler_params=_params(('parallel', 'parallel')),
        name=name,
    )(*args)


def _proj_cache_body(x_ref, w_ref, *rest, rope, b_rows):
    if rope:
        cos_ref, sin_ref = rest[:2]
        rest = rest[2:]
    ob_ref, oa_ref, og_ref = rest[-3:]
    tm = x_ref.shape[0]
    r = jnp.dot(x_ref[...], w_ref[...], preferred_element_type=F32)
    for g in range(r.shape[1] // LANES):
        sl = slice(g * LANES, (g + 1) * LANES)
        y = r[:, sl]
        if rope:
            y = y * cos_ref[...] + _swap_halves(y, rope) * sin_ref[...]
        ob_ref[:, sl] = y.astype(BF16)
        if g < A_KV_HEADS:
            oa_ref[pl.ds(g, tm, stride=A_KV_HEADS), :] = y
        else:
            og_ref[pl.ds(b_rows[g - A_KV_HEADS], tm, stride=len(b_rows)), :] = y


def proj_cache(x, w, bufs, layer, depth, *, b_rows, rope=None, tables=None, tm=1024, name='proj_cache'):
    M, K = x.shape
    _, N = w.shape
    tm = min(tm, M)
    nb = M // tm
    ng = len(b_rows)
    in_specs = [pl.BlockSpec((tm, K), lambda m: (m, 0)), pl.BlockSpec((K, N), lambda m: (0, 0))]
    args = [x, w]
    if rope:
        in_specs += [pl.BlockSpec((tm, LANES), lambda m: (m, 0))] * 2
        args += list(tables)
    aliases = {len(args): 1, len(args) + 1: 2}
    in_specs += [pl.BlockSpec(memory_space=pl.ANY)] * 2
    args += list(bufs)
    ob, buf_a, buf_g = pl.pallas_call(
        functools.partial(_proj_cache_body, rope=rope, b_rows=tuple(b_rows)),
        grid=(nb,),
        in_specs=in_specs,
        out_specs=[pl.BlockSpec((tm, N), lambda m: (m, 0)),
                   pl.BlockSpec((tm * A_KV_HEADS, LANES), lambda m: (layer * nb + m, 0)),
                   pl.BlockSpec((tm * ng, LANES), lambda m: (layer * nb + m, 0))],
        out_shape=[jax.ShapeDtypeStruct((M, N), BF16),
                   jax.ShapeDtypeStruct((depth * M * A_KV_HEADS, LANES), F32),
                   jax.ShapeDtypeStruct((depth * M * ng, LANES), F32)],
        input_output_aliases=aliases,
        compiler_params=_params(('arbitrary',)),
        name=name,
    )(*args)
    return ob, (buf_a, buf_g)


def _mm_act_body(x_ref, w_ref, o_ref, wb_ref):
    @pl.when(pl.program_id(1) == 0)
    def _():
        wb_ref[...] = w_ref[...].astype(BF16)

    r = jnp.dot(x_ref[...], wb_ref[...], preferred_element_type=F32)
    o_ref[...] = jnp.square(jnp.maximum(r, 0.0)).astype(o_ref.dtype)


def mlp_up(x, w, *, tm, tn):
    M, K = x.shape
    _, N = w.shape
    tm, tn = min(tm, M), min(tn, N)
    return pl.pallas_call(
        _mm_act_body,
        grid=(N // tn, M // tm),
        in_specs=[pl.BlockSpec((tm, K), lambda n, m: (m, 0)), pl.BlockSpec((K, tn), lambda n, m: (0, n))],
        out_specs=pl.BlockSpec((tm, tn), lambda n, m: (m, n)),
        out_shape=jax.ShapeDtypeStruct((M, N), BF16),
        scratch_shapes=[pltpu.VMEM((K, tn), BF16)],
        compiler_params=_params(('parallel', 'arbitrary')),
        name='mlp_up',
    )(x, w)


def _mm_ln_body(x_ref, w_ref, r_ref, g_ref, b_ref, o_ref, ob_ref, acc_ref, *, nk, alpha):
    k = pl.program_id(1)

    @pl.when(k == 0)
    def _():
        acc_ref[...] = jnp.zeros_like(acc_ref)

    acc_ref[...] += jnp.dot(x_ref[...], w_ref[...], preferred_element_type=F32)

    @pl.when(k == nk - 1)
    def _():
        y = alpha * r_ref[...] + acc_ref[...]
        mu = jnp.mean(y, axis=-1, keepdims=True)
        yc = y - mu
        var = jnp.mean(yc * yc, axis=-1, keepdims=True)
        out = yc * lax.rsqrt(var + LN_EPS) * g_ref[...] + b_ref[...]
        o_ref[...] = out
        ob_ref[...] = out.astype(BF16)


def matmul_ln(x, w, resid, g, b, *, alpha, tm, tk, name='mm_ln'):
    M, K = x.shape
    _, N = w.shape
    tm, tk = min(tm, M), min(tk, K)
    nk = K // tk
    return pl.pallas_call(
        functools.partial(_mm_ln_body, nk=nk, alpha=alpha),
        grid=(M // tm, nk),
        in_specs=[pl.BlockSpec((tm, tk), lambda m, k: (m, k)),
                  pl.BlockSpec((tk, N), lambda m, k: (k, 0)),
                  pl.BlockSpec((tm, N), lambda m, k: (m, 0)),
                  pl.BlockSpec((1, N), lambda m, k: (0, 0)),
                  pl.BlockSpec((1, N), lambda m, k: (0, 0))],
        out_specs=[pl.BlockSpec((tm, N), lambda m, k: (m, 0)),
                   pl.BlockSpec((tm, N), lambda m, k: (m, 0))],
        out_shape=[jax.ShapeDtypeStruct((M, N), F32), jax.ShapeDtypeStruct((M, N), BF16)],
        scratch_shapes=[pltpu.VMEM((tm, N), F32)],
        compiler_params=_params(('parallel', 'arbitrary')),
        name=name,
    )(x, w, resid, g.reshape(1, N), b.reshape(1, N))


def _branch_body(x_ref, a_ref, b_ref, c_ref, wb_ref, wga_ref, wgb_ref, wgc_ref, o_ref):
    x = x_ref[...]
    acc = None
    for n, (br, wg) in enumerate(((a_ref, wga_ref), (b_ref, wgb_ref), (c_ref, wgc_ref))):
        gate = jax.nn.sigmoid(jnp.dot(x, wg[...], preferred_element_type=F32))
        t = gate * jnp.dot(br[...], wb_ref[n], preferred_element_type=F32)
        acc = t if acc is None else acc + t
    o_ref[...] = acc.astype(o_ref.dtype)


def branch_merge(x16, brA, brB, brC, w_branch, w_gate, *, tm, tn):
    M = x16.shape[0]
    tm = min(tm, M)
    nb = D_MODEL // tn
    gspecs = [pl.BlockSpec((D_MODEL, tn), functools.partial(lambda m, n, base: (0, base + n), base=i * nb))
              for i in range(3)]
    return pl.pallas_call(
        _branch_body,
        grid=(M // tm, nb),
        in_specs=[pl.BlockSpec((tm, D_MODEL), lambda m, n: (m, 0))]
        + [pl.BlockSpec((tm, BRANCH_WIDTH), lambda m, n: (m, 0))] * 3
        + [pl.BlockSpec((3, BRANCH_WIDTH, tn), lambda m, n: (0, 0, n))] + gspecs,
        out_specs=pl.BlockSpec((tm, tn), lambda m, n: (m, n)),
        out_shape=jax.ShapeDtypeStruct((M, D_MODEL), BF16),
        compiler_params=_params(('parallel', 'parallel')),
        name='branch_merge',
    )(x16, brA, brB, brC, w_branch, w_gate, w_gate, w_gate)


def _dsa_prompt_body(qi_ref, misc_ref, kz_ref, qa_ref, ka_ref, va_ref, o_ref, keys_ref, bias_ref, hi_ref, lo_ref,
                     *, top_k, tk, T, qrows):
    QB = qrows
    qb = pl.program_id(1)
    q0 = qb * QB
    nkt = (q0 + QB - 1) // tk + 1
    qpos = q0 + lax.broadcasted_iota(I32, (QB, tk), 0)
    lane = lax.broadcasted_iota(I32, (QB, tk), 1)
    qpos_c = q0 + lax.broadcasted_iota(I32, (QB, LANES), 0)
    lane_c = lax.broadcasted_iota(I32, (QB, LANES), 1)
    wi = misc_ref[:, MISC_WI:MISC_WI + IDX_HEADS]

    def score_tile(j, carry):
        off = pl.multiple_of(j * tk, tk)
        kz = (kz_ref[pl.ds(off, tk), :LANES], kz_ref[pl.ds(off, tk), LANES:])
        acc = jnp.zeros((QB, tk), F32)
        for h in range(IDX_HEADS):
            qp = qi_ref[:, (h // 2) * LANES:(h // 2 + 1) * LANES]
            s = lax.dot_general(qp, kz[h % 2], NT_DIMS, preferred_element_type=F32)
            acc = acc + wi[:, h:h + 1] * jnp.maximum(s, 0.0)
        key = jnp.where(off + lane <= qpos, _sortable(acc), INT_MIN)
        keys_ref[:, pl.ds(off, tk)] = key
        hi_ref[:, pl.ds(off, tk)] = (key >> 16).astype(I16)
        return carry

    lax.fori_loop(0, nkt, score_tile, 0)

    def count16(ref, pred):
        def body(j, acc):
            off = pl.multiple_of(j * tk, tk)
            for c in range(tk // LANES):
                kc = ref[:, pl.ds(off + c * LANES, LANES)]
                acc = acc + jnp.where(pred(kc), jnp.int16(1), jnp.int16(0))
            return acc
        acc = lax.fori_loop(0, nkt, body, jnp.zeros((QB, LANES), I16))
        return jnp.sum(acc.astype(F32), axis=1, keepdims=True)

    def bcast16(v):
        return jnp.broadcast_to(v, (QB, LANES)).astype(I16)

    def kth_largest16(ref, kneed):
        zero_b = jnp.zeros((QB, LANES), I16)
        lo = jnp.where(count16(ref, lambda kc: kc >= zero_b) >= kneed, jnp.int32(0), jnp.int32(-(2 ** 15)))

        def bit_body(i, lo):
            cand = lo + (jnp.int32(1) << (14 - i))
            cand_b = bcast16(cand)
            return jnp.where(count16(ref, lambda kc: kc >= cand_b) >= kneed, cand, lo)

        return lax.fori_loop(0, 15, bit_body, lo)

    def count(pred):
        def body(j, acc):
            off = pl.multiple_of(j * tk, tk)
            for c in range(tk // LANES):
                kc = keys_ref[:, pl.ds(off + c * LANES, LANES)]
                acc = acc + jnp.where(pred(kc, off + c * LANES + lane_c), 1.0, 0.0)
            return acc
        acc = lax.fori_loop(0, nkt, body, jnp.zeros((QB, LANES), F32))
        return jnp.sum(acc, axis=1, keepdims=True)

    def bcast(v):
        return jnp.broadcast_to(v, (QB, LANES))

    kf = float(top_k)
    t_hi = kth_largest16(hi_ref, kf)
    t_hi16 = bcast16(t_hi)
    need_lo = kf - count16(hi_ref, lambda kc: kc > t_hi16)
    t_hi_b = bcast(t_hi)

    def low_tile(j, carry):
        off = pl.multiple_of(j * tk, tk)
        for c in range(tk // LANES):
            kc = keys_ref[:, pl.ds(off + c * LANES, LANES)]
            lo16 = (kc & jnp.int32(0xFFFF)) - jnp.int32(2 ** 15)
            lo_ref[:, pl.ds(off + c * LANES, LANES)] = jnp.where((kc >> 16) == t_hi_b, lo16, jnp.int32(-(2 ** 15))).astype(I16)
        return carry

    lax.fori_loop(0, nkt, low_tile, 0)
    t_lo = kth_largest16(lo_ref, need_lo)
    thr = t_hi * jnp.int32(2 ** 16) + (t_lo + jnp.int32(2 ** 15))
    thr_b = bcast(thr)

    need = kf - count(lambda kc, kp: kc > thr_b)
    ceq = count(lambda kc, kp: kc == thr_b)
    nbits = max(T.bit_length(), 1)

    def tie_fn():
        def jb(i, j):
            cand = j + (jnp.int32(1) << (nbits - 1 - i))
            cand_b = bcast(cand)
            g = count(lambda kc, kp: (kc == thr_b) & (kp < cand_b))
            return jnp.where(g <= need, cand, j)
        return lax.fori_loop(0, nbits, jb, jnp.zeros((QB, 1), I32))

    jstar = lax.cond(jnp.max(ceq - need) > 0.0, tie_fn, lambda: jnp.full((QB, 1), 2 ** 30, I32))
    jstar_b = bcast(jstar)

    def bias_tile(j, carry):
        off = pl.multiple_of(j * tk, tk)
        for c in range(tk // LANES):
            kc = keys_ref[:, pl.ds(off + c * LANES, LANES)]
            kp = off + c * LANES + lane_c
            sel = ((kc > thr_b) | ((kc == thr_b) & (kp < jstar_b))) & (kp <= qpos_c)
            bias_ref[:, pl.ds(off + c * LANES, LANES)] = jnp.where(sel, 0.0, NEG)
        return carry

    lax.fori_loop(0, nkt, bias_tile, 0)

    AB = Q_BLOCK
    for sb in range(QB // AB):
        rs = slice(sb * AB, (sb + 1) * AB)
        nkt_s = (q0 + (sb + 1) * AB - 1) // tk + 1
        qgs = [jnp.concatenate([qa_ref[rs, (A_REP * g + r) * HEAD_DIM:(A_REP * g + r + 1) * HEAD_DIM]
                                for r in range(A_REP)], axis=0) for g in range(A_KV_HEADS)]

        def att_tile(j, carry, rs=rs, qgs=qgs):
            off = pl.multiple_of(j * tk, tk)
            b = bias_ref[rs, pl.ds(off, tk)]
            new = []
            for g in range(A_KV_HEADS):
                m, l, acc = carry[g]
                gs = slice(g * HEAD_DIM, (g + 1) * HEAD_DIM)
                kt = ka_ref[pl.ds(off, tk), gs]
                vt = va_ref[pl.ds(off, tk), gs]
                s = lax.dot_general(qgs[g], kt, NT_DIMS, preferred_element_type=F32)
                s = (s.reshape(A_REP, AB, tk) + b[None]).reshape(A_REP * AB, tk)
                m_new = jnp.maximum(m, jnp.max(s, axis=1, keepdims=True))
                corr = jnp.exp(m - m_new)
                p = jnp.exp(s - m_new)
                l = l * corr + jnp.sum(p, axis=1, keepdims=True)
                acc = acc * corr + jnp.dot(p.astype(BF16), vt, preferred_element_type=F32)
                new.append((m_new, l, acc))
            return tuple(new)

        one = (jnp.full((A_REP * AB, 1), NEG, F32), jnp.zeros((A_REP * AB, 1), F32),
               jnp.zeros((A_REP * AB, HEAD_DIM), F32))
        carry = lax.fori_loop(0, nkt_s // 2, lambda j2, c, f=att_tile: f(2 * j2 + 1, f(2 * j2, c)),
                              (one,) * A_KV_HEADS)
        carry = lax.fori_loop(2 * (nkt_s // 2), nkt_s, att_tile, carry)
        for g in range(A_KV_HEADS):
            _, l, acc = carry[g]
            out = acc / l
            for r in range(A_REP):
                h = A_REP * g + r
                o_ref[rs, h * HEAD_DIM:(h + 1) * HEAD_DIM] = out[r * AB:(r + 1) * AB].astype(o_ref.dtype)


def dsa_prompt(qi, misc, kz, q128, k128, v16):
    B, T, _ = qi.shape
    top_k = min(TOPK_MAX, T // 4)
    tk = min(512, T)
    qrows = min(2 * Q_BLOCK, T)
    nb = T // qrows
    kvw = A_KV_HEADS * HEAD_DIM
    return pl.pallas_call(
        functools.partial(_dsa_prompt_body, top_k=top_k, tk=tk, T=T, qrows=qrows),
        grid=(B, nb),
        in_specs=[pl.BlockSpec((None, qrows, IDX_HEADS * IDX_DIM), lambda b, q: (b, q, 0)),
                  pl.BlockSpec((None, qrows, LANES), lambda b, q: (b, q, 0)),
                  pl.BlockSpec((None, T, 2 * LANES), lambda b, q: (b, 0, 0)),
                  pl.BlockSpec((None, qrows, A_HEADS * HEAD_DIM), lambda b, q: (b, q, 0)),
                  pl.BlockSpec((None, T, kvw), lambda b, q: (b, 0, 0)),
                  pl.BlockSpec((None, T, kvw), lambda b, q: (b, 0, 0))],
        out_specs=pl.BlockSpec((None, qrows, A_HEADS * HEAD_DIM), lambda b, q: (b, q, 0)),
        out_shape=jax.ShapeDtypeStruct((B, T, A_HEADS * HEAD_DIM), BF16),
        scratch_shapes=[pltpu.VMEM((qrows, T), I32), pltpu.VMEM((qrows, T), F32),
                        pltpu.VMEM((qrows, T), I16), pltpu.VMEM((qrows, T), I16)],
        compiler_params=_params(('parallel', 'arbitrary')),
        name='dsa_prompt',
    )(qi, misc, kz, q128, k128, v16)


def _lam(lq1, lk1, lq2, lk2, lam_init):
    return (jnp.exp(jnp.sum(lq1[...] * lk1[...], axis=-1, keepdims=True))
            - jnp.exp(jnp.sum(lq2[...] * lk2[...], axis=-1, keepdims=True)) + lam_init)


def _diff_finish(o0, o1, lam, subln, lam_init):
    a = o0 - lam * o1
    a = a * lax.rsqrt(jnp.mean(a * a, axis=-1, keepdims=True) + RMS_EPS) * subln
    return a * (1.0 - lam_init)


def _diff_prompt_body(lq1, lk1, lq2, lk2, sub_ref, q_ref, k_ref, v_ref, o_ref, *, tq, tk, lam_init):
    qi = pl.program_id(2)
    q0 = qi * tq
    n_full = q0 // tk
    n_all = (q0 + tq - 1) // tk + 1
    qpos = q0 + lax.broadcasted_iota(I32, (tq, tk), 0)
    lane = lax.broadcasted_iota(I32, (tq, tk), 1)
    qs = [q_ref[:, mp * HEAD_DIM:(mp + 1) * HEAD_DIM] for mp in range(2)]

    def tile(j, carry, masked):
        off = pl.multiple_of(j * tk, tk)
        vt = v_ref[pl.ds(off, tk), :]
        new = []
        for mp in range(2):
            m, l, acc = carry[mp]
            kt = k_ref[pl.ds(off, tk), mp * HEAD_DIM:(mp + 1) * HEAD_DIM]
            s = lax.dot_general(qs[mp], kt, NT_DIMS, preferred_element_type=F32)
            if masked:
                s = jnp.where(off + lane <= qpos, s, NEG)
            m_new = jnp.maximum(m, jnp.max(s, axis=1, keepdims=True))
            corr = jnp.exp(m - m_new)
            p = jnp.exp(s - m_new)
            l = l * corr + jnp.sum(p, axis=1, keepdims=True)
            acc = acc * corr + jnp.dot(p.astype(BF16), vt, preferred_element_type=F32)
            new.append((m_new, l, acc))
        return tuple(new)

    one = (jnp.full((tq, 1), NEG, F32), jnp.zeros((tq, 1), F32), jnp.zeros((tq, B_V_DIM), F32))
    def tile2(j2, carry):
        return tile(2 * j2 + 1, tile(2 * j2, carry, masked=False), masked=False)

    carry = lax.fori_loop(0, n_full // 2, tile2, (one, one))
    carry = lax.fori_loop(2 * (n_full // 2), n_full, functools.partial(tile, masked=False), carry)
    carry = lax.fori_loop(n_full, n_all, functools.partial(tile, masked=True), carry)
    outs = [acc / l for (_, l, acc) in carry]
    lam = _lam(lq1, lk1, lq2, lk2, lam_init)
    o_ref[...] = _diff_finish(outs[0], outs[1], lam, sub_ref[...], lam_init).astype(o_ref.dtype)


def diff_prompt(q128, k128, v16, lam_q1, lam_k1, lam_q2, lam_k2, subln, lam_init):
    B, T, _ = q128.shape
    H = B_HEADS
    tq = min(512, T)
    tk = min(512, T)
    pw = 2 * HEAD_DIM
    qb0 = (A_HEADS * HEAD_DIM) // pw
    kb0 = (A_KV_HEADS * HEAD_DIM) // pw
    vec = pl.BlockSpec((1, HEAD_DIM), lambda b, h, q: (0, 0))
    return pl.pallas_call(
        functools.partial(_diff_prompt_body, tq=tq, tk=tk, lam_init=lam_init),
        grid=(B, H, T // tq),
        in_specs=[vec, vec, vec, vec,
                  pl.BlockSpec((1, B_V_DIM), lambda b, h, q: (0, 0)),
                  pl.BlockSpec((None, tq, pw), lambda b, h, q: (b, q, qb0 + h)),
                  pl.BlockSpec((None, T, pw), lambda b, h, q: (b, 0, kb0 + h)),
                  pl.BlockSpec((None, T, B_V_DIM), lambda b, h, q: (b, 0, kb0 + h))],
        out_specs=pl.BlockSpec((None, tq, B_V_DIM), lambda b, h, q: (b, q, h)),
        out_shape=jax.ShapeDtypeStruct((B, T, H * B_V_DIM), BF16),
        compiler_params=_params(('parallel', 'parallel', 'arbitrary')),
        name='diff_prompt',
    )(lam_q1.reshape(1, -1), lam_k1.reshape(1, -1), lam_q2.reshape(1, -1), lam_k2.reshape(1, -1),
      subln.reshape(1, -1), q128, k128, v16)


def _gla_body(q_ref, k_ref, v_ref, rc_ref, misc_ref, w2_ref, gb_ref, gn_ref, s0_ref, o_ref, sfin_ref, st_ref,
              *, tb, chunk, t_valid, nt):
    t = pl.program_id(1)

    @pl.when(t == 0)
    def _():
        st_ref[...] = s0_ref[...]

    x = jnp.dot(misc_ref[...].astype(BF16), w2_ref[...], preferred_element_type=F32) + gb_ref[...]
    la = (jnp.minimum(x, 0.0) - jnp.log(1.0 + jnp.exp(-jnp.abs(x)))) * (1.0 / GLA_TAU)
    W = C_HEADS * C_K_DIM
    row = lax.broadcasted_iota(I32, (tb, W), 0)
    if t_valid is not None:
        la = jnp.where(t * tb + row < t_valid, la, 0.0)
    rowc = row % chunk
    b = la
    sh = 1
    while sh < chunk:
        b = b + jnp.where(rowc >= sh, pltpu.roll(b, sh, axis=0), 0.0)
        sh *= 2
    k = k_ref[...]
    qe = (q_ref[...] * (C_K_DIM ** -0.5) * jnp.exp(b)).astype(BF16)
    ke = (k * jnp.exp(-b)).astype(BF16)
    v16 = v_ref[...].astype(BF16)
    rc = rc_ref[...]
    gn = gn_ref[...]
    tril = lax.broadcasted_iota(I32, (chunk, chunk), 0) >= lax.broadcasted_iota(I32, (chunk, chunk), 1)
    for c in range(tb // chunk):
        r0 = c * chunk
        bl = b[r0 + chunk - 1:r0 + chunk, :]
        kd = (k[r0:r0 + chunk] * jnp.exp(bl - b[r0:r0 + chunk])).astype(BF16)
        dec = jnp.exp(bl)
        for h in range(C_HEADS):
            ck = slice(h * C_K_DIM, (h + 1) * C_K_DIM)
            cv = slice(h * C_V_DIM, (h + 1) * C_V_DIM)
            qe_c = qe[r0:r0 + chunk, ck]
            v_c = v16[r0:r0 + chunk, cv]
            att = lax.dot_general(qe_c, ke[r0:r0 + chunk, ck], NT_DIMS, preferred_element_type=F32)
            att = jnp.where(tril, att, 0.0).astype(BF16)
            st = st_ref[h]
            o_c = (lax.dot_general(qe_c, st.astype(BF16), NT_DIMS, preferred_element_type=F32)
                   + jnp.dot(att, v_c, preferred_element_type=F32))
            st_ref[h] = st * dec[:, ck] + lax.dot_general(v_c, kd[:, ck], TN_DIMS, preferred_element_type=F32)
            o_n = o_c * lax.rsqrt(jnp.mean(o_c * o_c, axis=-1, keepdims=True) + RMS_EPS) * gn
            r_c = rc[r0:r0 + chunk, cv]
            o_ref[r0:r0 + chunk, cv] = (o_n * (r_c * jax.nn.sigmoid(r_c))).astype(o_ref.dtype)

    @pl.when(t == nt - 1)
    def _():
        sfin_ref[...] = st_ref[...]


def gla(cproj, misc, w2pad, gb, gn, s0t, *, t_valid=None):
    B, T, _ = cproj.shape
    chunk = GLA_CHUNK
    tb = min(256, T)
    nt = T // tb
    W = C_HEADS * C_K_DIM
    V = C_HEADS * C_V_DIM
    st_spec = pl.BlockSpec((None, C_HEADS, C_V_DIM, C_K_DIM), lambda b, t: (b, 0, 0, 0))
    return pl.pallas_call(
        functools.partial(_gla_body, tb=tb, chunk=chunk, t_valid=t_valid, nt=nt),
        grid=(B, nt),
        in_specs=[pl.BlockSpec((None, tb, W), lambda b, t: (b, t, 0)),
                  pl.BlockSpec((None, tb, W), lambda b, t: (b, t, 1)),
                  pl.BlockSpec((None, tb, V), lambda b, t: (b, t, 1)),
                  pl.BlockSpec((None, tb, V), lambda b, t: (b, t, 2)),
                  pl.BlockSpec((None, tb, LANES), lambda b, t: (b, t, 0)),
                  pl.BlockSpec((LANES, W), lambda b, t: (0, 0)),
                  pl.BlockSpec((1, W), lambda b, t: (0, 0)),
                  pl.BlockSpec((1, C_V_DIM), lambda b, t: (0, 0)),
                  st_spec],
        out_specs=[pl.BlockSpec((None, tb, V), lambda b, t: (b, t, 0)), st_spec],
        out_shape=[jax.ShapeDtypeStruct((B, T, V), BF16),
                   jax.ShapeDtypeStruct((B, C_HEADS, C_V_DIM, C_K_DIM), F32)],
        scratch_shapes=[pltpu.VMEM((C_HEADS, C_V_DIM, C_K_DIM), F32)],
        compiler_params=_params(('parallel', 'arbitrary')),
        name='gla',
    )(cproj, cproj, cproj, cproj, misc, w2pad, gb.reshape(1, W), gn.reshape(1, C_V_DIM), s0t)


def _page_specs(block, layer, pps):
    def mk(i):
        return pl.BlockSpec((None, None) + block,
                            lambda b, p, pt: (layer, pt[b, p * pps + i]) + (0,) * len(block))
    return [mk(i) for i in range(pps)]


def _dsa_sample_score_body(pt_ref, qi_ref, wi_ref, kin_ref, *rest, pps, n_pages):
    pages, o_ref = rest[:pps], rest[pps]
    p = pl.program_id(1)
    qi = qi_ref[...]
    wi = wi_ref[...]

    @pl.when(p == 0)
    def _():
        o_ref[...] = jnp.full(o_ref.shape, -jnp.inf, F32)
        s = jnp.sum(qi.astype(F32) * kin_ref[...], axis=1, keepdims=True)
        snew = jnp.sum(wi * jnp.maximum(s, 0.0), axis=0, keepdims=True)
        lane = lax.broadcasted_iota(I32, (1, PAGE_SIZE), 1)
        o_ref[n_pages:n_pages + 1, :] = jnp.where(lane == 0, snew, -jnp.inf)

    kpt = jnp.concatenate([pg[...].astype(BF16) for pg in pages], axis=1)
    s = jnp.dot(qi, kpt, preferred_element_type=F32)
    sc = jnp.sum(wi * jnp.maximum(s, 0.0), axis=0, keepdims=True)
    for i in range(pps):
        o_ref[pl.ds(p * pps + i, 1), :] = sc[:, i * PAGE_SIZE:(i + 1) * PAGE_SIZE]


def _topk_bias(sc, top_k, n_valid):
    R = sc.shape[0]
    key = _sortable(sc)
    pos = lax.broadcasted_iota(I32, sc.shape, 0) * PAGE_SIZE + lax.broadcasted_iota(I32, sc.shape, 1)
    key = jnp.where(pos < n_valid, key, INT_MIN)

    def count(hit):
        c = jnp.sum(jnp.where(hit, 1.0, 0.0), axis=1, keepdims=True)
        return jnp.sum(c, axis=0, keepdims=True)

    kf = float(top_k)
    lo = jnp.where(count(key >= 0) >= kf, jnp.int32(0), jnp.int32(INT_MIN))

    def bit_body(i, lo):
        cand = lo + (jnp.int32(1) << (30 - i))
        return jnp.where(count(key >= cand) >= kf, cand, lo)

    thr = lax.fori_loop(0, 31, bit_body, lo)
    need = kf - count(key > thr)
    nbits = (R * PAGE_SIZE).bit_length()

    def jb(i, j):
        cand = j + (jnp.int32(1) << (nbits - 1 - i))
        return jnp.where(count((key == thr) & (pos < cand)) <= need, cand, j)

    jstar = lax.fori_loop(0, nbits, jb, jnp.zeros((1, 1), I32))
    sel = ((key > thr) | ((key == thr) & (pos < jstar))) & (pos < n_valid)
    return jnp.where(sel, 0.0, NEG)


def _dsa_sample_attn_body(pt_ref, sc_ref, q_ref, kn_ref, vn_ref, *rest, pps, n_pages, top_k):
    kpages, vpages = rest[:pps], rest[pps:2 * pps]
    o_ref, bias_ref, m_ref, l_ref, acc_ref = rest[2 * pps:]
    p = pl.program_id(1)
    q = q_ref[...]
    first = lax.broadcasted_iota(I32, (A_HEADS, PAGE_SIZE), 0) < A_REP

    @pl.when(p == 0)
    def _():
        bias_ref[...] = _topk_bias(sc_ref[...], top_k, n_pages * PAGE_SIZE + 1)
        m_ref[...] = jnp.full(m_ref.shape, NEG, F32)
        l_ref[...] = jnp.zeros_like(l_ref)
        acc_ref[...] = jnp.zeros_like(acc_ref)

    def kv(refs, g):
        return jnp.concatenate([r[pl.ds(g, PAGE_SIZE, stride=A_KV_HEADS), :].astype(BF16) for r in refs], axis=0)

    W = pps * PAGE_SIZE
    first_w = lax.broadcasted_iota(I32, (A_HEADS, W), 0) < A_REP
    s0 = lax.dot_general(q, kv(kpages, 0), NT_DIMS, preferred_element_type=F32)
    s1 = lax.dot_general(q, kv(kpages, 1), NT_DIMS, preferred_element_type=F32)
    bias = jnp.concatenate([bias_ref[pl.ds(p * pps + i, 1), :] for i in range(pps)], axis=1)
    s = jnp.where(first_w, s0, s1) + bias
    m = m_ref[...]
    m_new = jnp.maximum(m, jnp.max(s, axis=1, keepdims=True))
    corr = jnp.exp(m - m_new)
    pr = jnp.exp(s - m_new)
    l_ref[...] = l_ref[...] * corr + jnp.sum(pr, axis=1, keepdims=True)
    pb = pr.astype(BF16)
    pv = jnp.where(first, jnp.dot(pb, kv(vpages, 0), preferred_element_type=F32),
                   jnp.dot(pb, kv(vpages, 1), preferred_element_type=F32))
    acc_ref[...] = acc_ref[...] * corr + pv
    m_ref[...] = m_new

    @pl.when(p == pl.num_programs(1) - 1)
    def _():
        s = (jnp.sum(q.astype(F32) * kn_ref[...], axis=1, keepdims=True)
             + bias_ref[n_pages:n_pages + 1, 0:1])
        m = m_ref[...]
        m_new = jnp.maximum(m, s)
        corr = jnp.exp(m - m_new)
        pr = jnp.exp(s - m_new)
        l = l_ref[...] * corr + pr
        o_ref[...] = (acc_ref[...] * corr + pr * vn_ref[...]) / l


def dsa_sample(layer, page_table, qi, wi, ki_new, qa, ka_new, va_new, cache_idx_kt, cache_a_k, cache_a_v):
    DB, n_pages = page_table.shape
    pps_s = math.gcd(32, n_pages)
    pps = math.gcd(16, n_pages)
    R = ((n_pages + 1 + 7) // 8) * 8
    L = n_pages * PAGE_SIZE + 1
    top_k = min(TOPK_MAX, L // 4)
    per_b = lambda *blk: pl.BlockSpec((None,) + blk, lambda b, p, pt: (b,) + (0,) * len(blk))
    scores = pl.pallas_call(
        functools.partial(_dsa_sample_score_body, pps=pps_s, n_pages=n_pages),
        grid_spec=pltpu.PrefetchScalarGridSpec(
            num_scalar_prefetch=1, grid=(DB, n_pages // pps_s),
            in_specs=[per_b(IDX_HEADS, IDX_DIM), per_b(IDX_HEADS, 1), per_b(1, IDX_DIM)]
            + _page_specs((IDX_DIM, PAGE_SIZE), layer, pps_s),
            out_specs=per_b(R, PAGE_SIZE)),
        out_shape=jax.ShapeDtypeStruct((DB, R, PAGE_SIZE), F32),
        compiler_params=_params(('parallel', 'arbitrary')),
        name='dsa_sample_scores',
    )(page_table, qi, wi, ki_new, *([cache_idx_kt] * pps_s))
    rows = PAGE_SIZE * A_KV_HEADS
    return pl.pallas_call(
        functools.partial(_dsa_sample_attn_body, pps=pps, n_pages=n_pages, top_k=top_k),
        grid_spec=pltpu.PrefetchScalarGridSpec(
            num_scalar_prefetch=1, grid=(DB, n_pages // pps),
            in_specs=[per_b(R, PAGE_SIZE), per_b(A_HEADS, HEAD_DIM), per_b(A_HEADS, HEAD_DIM), per_b(A_HEADS, HEAD_DIM)]
            + _page_specs((rows, HEAD_DIM), layer, pps) + _page_specs((rows, HEAD_DIM), layer, pps),
            out_specs=per_b(A_HEADS, HEAD_DIM),
            scratch_shapes=[pltpu.VMEM((R, PAGE_SIZE), F32), pltpu.VMEM((A_HEADS, 1), F32),
                            pltpu.VMEM((A_HEADS, 1), F32), pltpu.VMEM((A_HEADS, HEAD_DIM), F32)]),
        out_shape=jax.ShapeDtypeStruct((DB, A_HEADS, HEAD_DIM), F32),
        compiler_params=_params(('parallel', 'arbitrary')),
        name='dsa_sample_attn',
    )(page_table, scores, qa, ka_new, va_new, *([cache_a_k] * pps), *([cache_a_v] * pps))


def _diff_sample_body(pt_ref, lq1, lk1, lq2, lk2, sub_ref, q_ref, kn_ref, vn_ref, *rest, pps, lam_init):
    kpages, vpages = rest[:pps], rest[pps:2 * pps]
    o_ref, m_ref, l_ref, acc_ref = rest[2 * pps:]
    p = pl.program_id(1)
    NJ = 2 * B_HEADS
    q = q_ref[...]
    rowk = lax.broadcasted_iota(I32, (NJ, PAGE_SIZE), 0)

    @pl.when(p == 0)
    def _():
        m_ref[...] = jnp.full(m_ref.shape, NEG, F32)
        l_ref[...] = jnp.zeros_like(l_ref)
        acc_ref[...] = jnp.zeros_like(acc_ref)

    def rows(refs, j):
        return jnp.concatenate([r[pl.ds(j, PAGE_SIZE, stride=NJ), :].astype(BF16) for r in refs], axis=0)

    W = pps * PAGE_SIZE
    roww = lax.broadcasted_iota(I32, (NJ, W), 0)
    s = jnp.zeros((NJ, W), F32)
    for j in range(NJ):
        sj = lax.dot_general(q, rows(kpages, j), NT_DIMS, preferred_element_type=F32)
        s = jnp.where(roww == j, sj, s)
    m = m_ref[...]
    m_new = jnp.maximum(m, jnp.max(s, axis=1, keepdims=True))
    corr = jnp.exp(m - m_new)
    pr = jnp.exp(s - m_new)
    l_ref[...] = l_ref[...] * corr + jnp.sum(pr, axis=1, keepdims=True)
    pb = pr.astype(BF16)
    halves = []
    for c in range(B_V_DIM // LANES):
        pv = jnp.zeros((NJ, LANES), F32)
        for h in range(B_HEADS):
            ph = jnp.dot(pb, rows(vpages, c * B_HEADS + h), preferred_element_type=F32)
            pv = jnp.where(rowk // 2 == h, ph, pv)
        halves.append(pv)
    acc_ref[...] = acc_ref[...] * corr + jnp.concatenate(halves, axis=1)
    m_ref[...] = m_new

    @pl.when(p == pl.num_programs(1) - 1)
    def _():
        s = jnp.sum(q.astype(F32) * kn_ref[...], axis=1, keepdims=True)
        m = m_ref[...]
        m_new = jnp.maximum(m, s)
        corr = jnp.exp(m - m_new)
        pr = jnp.exp(s - m_new)
        l = l_ref[...] * corr + pr
        o = (acc_ref[...] * corr + pr * vn_ref[...]) / l
        lam = _lam(lq1, lk1, lq2, lk2, lam_init)
        for h in range(B_HEADS):
            o_ref[h:h + 1, :] = _diff_finish(o[2 * h:2 * h + 1], o[2 * h + 1:2 * h + 2], lam, sub_ref[...], lam_init)


def diff_sample(layer, page_table, qb, kb_new, vb_new, cache_b_k, cache_b_v,
                lam_q1, lam_k1, lam_q2, lam_k2, subln, lam_init):
    DB, n_pages = page_table.shape
    pps = math.gcd(8, n_pages)
    NJ = 2 * B_HEADS
    per_b = lambda *blk: pl.BlockSpec((None,) + blk, lambda b, p, pt: (b,) + (0,) * len(blk))
    vec = lambda w: pl.BlockSpec((1, w), lambda b, p, pt: (0, 0))
    return pl.pallas_call(
        functools.partial(_diff_sample_body, pps=pps, lam_init=lam_init),
        grid_spec=pltpu.PrefetchScalarGridSpec(
            num_scalar_prefetch=1, grid=(DB, n_pages // pps),
            in_specs=[vec(HEAD_DIM)] * 4 + [vec(B_V_DIM),
                      per_b(NJ, HEAD_DIM), per_b(NJ, HEAD_DIM), per_b(NJ, B_V_DIM)]
            + _page_specs((PAGE_SIZE * NJ, HEAD_DIM), layer, pps)
            + _page_specs((PAGE_SIZE * NJ, LANES), layer, pps),
            out_specs=per_b(B_HEADS, B_V_DIM),
            scratch_shapes=[pltpu.VMEM((NJ, 1), F32), pltpu.VMEM((NJ, 1), F32), pltpu.VMEM((NJ, B_V_DIM), F32)]),
        out_shape=jax.ShapeDtypeStruct((DB, B_HEADS, B_V_DIM), F32),
        compiler_params=_params(('parallel', 'arbitrary')),
        name='diff_sample',
    )(page_table, lam_q1.reshape(1, -1), lam_k1.reshape(1, -1), lam_q2.reshape(1, -1), lam_k2.reshape(1, -1),
      subln.reshape(1, -1), qb, kb_new, vb_new, *([cache_b_k] * pps), *([cache_b_v] * pps))


def _split_w_in(w):
    src = {}
    off = 0
    for name, width in SRC_SEGMENTS:
        src[name] = w[:, off:off + width]
        off += width
    cat = lambda *names: jnp.concatenate([src[n] for n in names], axis=1).astype(BF16)
    pad = jnp.zeros((w.shape[0], LANES - IDX_DIM - IDX_HEADS - GLA_RANK), w.dtype)
    return {
        'q128': cat('qa', 'qb'), 'k128': cat('ka', 'kb'), 'qi': cat('qi'), 'v': cat('va', 'vb'),
        'c': cat('qc', 'kc', 'vc', 'rc'), 'gate': cat('gate'),
        'misc': jnp.concatenate([src['ki'], src['wi'], src['gc'], pad], axis=1).astype(BF16),
    }


def _rope_tables(pos):
    def tab(dh):
        half = dh // 2
        inv_freq = ROPE_THETA ** (-jnp.arange(half, dtype=F32) / half)
        ang = pos.astype(F32)[:, None] * inv_freq[None, :]
        c, s = jnp.cos(ang), jnp.sin(ang)
        reps = LANES // dh
        return jnp.tile(jnp.concatenate([c, c], axis=1), (1, reps)), jnp.tile(jnp.concatenate([-s, s], axis=1), (1, reps))
    c128, s128 = tab(HEAD_DIM)
    c64, s64 = tab(IDX_DIM)
    n = pos.shape[0]
    tail_c = jnp.concatenate([jnp.full((n, IDX_HEADS), IDX_HEADS ** -0.5, F32),
                              jnp.ones((n, LANES - IDX_DIM - IDX_HEADS), F32)], axis=1)
    cm = jnp.concatenate([c64[:, :IDX_DIM], tail_c], axis=1)
    sm = jnp.concatenate([s64[:, :IDX_DIM], jnp.zeros((n, LANES - IDX_DIM), F32)], axis=1)
    return (c128, s128), (c64, s64), (cm, sm)


BK_ROWS = tuple(range(2 * B_HEADS))
BV_ROWS = tuple((g % 2) * B_HEADS + g // 2 for g in range(2 * B_HEADS))


def _project(x16, wg, tabs, cache=None, layer=0):
    t128, t64, tm_ = tabs
    q128, = proj(x16, wg['q128'], out_dtypes=(BF16,), rope=HEAD_DIM, tables=t128, scale=HEAD_DIM ** -0.5, name='proj_q128')
    qi, = proj(x16, wg['qi'], out_dtypes=(BF16,), rope=IDX_DIM, tables=t64, scale=IDX_DIM ** -0.5, name='proj_qi')
    misc, kz = proj(x16, wg['misc'], out_dtypes=(F32,), rope=IDX_DIM, tables=tm_, emit_kz=True, name='proj_misc')
    cproj, = proj(x16, wg['c'], out_dtypes=(F32,), name='proj_c')
    out = dict(q128=q128, qi=qi, misc=misc, kz=kz, c=cproj)
    if cache is None:
        out['k128f'], out['k128'] = proj(x16, wg['k128'], out_dtypes=(F32, BF16), rope=HEAD_DIM, tables=t128, tn=1280,
                                         name='proj_k128')
        out['vf'], out['v16'] = proj(x16, wg['v'], out_dtypes=(F32, BF16), tn=1280, name='proj_v')
        return out, None
    depth = cache['depth']
    out['k128'], (ak, bk) = proj_cache(x16, wg['k128'], (cache['a_k'], cache['b_k']), layer, depth, b_rows=BK_ROWS,
                                       rope=HEAD_DIM, tables=t128, name='proj_k128c')
    out['v16'], (av, bv) = proj_cache(x16, wg['v'], (cache['a_v'], cache['b_v']), layer, depth, b_rows=BV_ROWS,
                                      name='proj_vc')
    return out, dict(a_k=ak, b_k=bk, a_v=av, b_v=bv, depth=depth)


def _tail(x, x16, oA, oB, oC, lw, alpha):
    g = branch_merge(x16, oA, oB, oC, lw['w_branch'], lw['w_gate'], tm=1024, tn=512)
    x1, x1b = matmul_ln(g, lw['w_out'], x, lw['ln1_g'], lw['ln1_b'], alpha=alpha, tm=512, tk=2048, name='out_ln1')
    hid = mlp_up(x1b, lw['w_up'], tm=1024, tn=1024)
    return matmul_ln(hid, lw['w_down'], x1, lw['ln2_g'], lw['ln2_b'], alpha=alpha, tm=512, tk=2048, name='down_ln2')


def kernel(x_prompt, x_sample, cache_a_k, cache_a_v, cache_idx_k, cache_b_k, cache_b_v, state_gla, page_table,
           w_in, gla_w2, gla_b, lam_q1, lam_k1, lam_q2, lam_k2, diff_subln, gla_norm, w_branch, w_out,
           ln1_g, ln1_b, w_up, w_down, ln2_g, ln2_b):
    B, T, D = x_prompt.shape
    DB, Ts, _ = x_sample.shape
    assert Ts == 1
    DEPTH = w_in.shape[0]
    n_pages = page_table.shape[1]
    P = n_pages * PAGE_SIZE
    alpha = (2.0 * DEPTH) ** 0.25
    n_phys = cache_a_k.shape[1]
    ca_k = cache_a_k.reshape(DEPTH, n_phys, PAGE_SIZE * A_KV_HEADS, HEAD_DIM)
    ca_v = cache_a_v.reshape(DEPTH, n_phys, PAGE_SIZE * A_KV_HEADS, HEAD_DIM)
    cb_k = cache_b_k.reshape(DEPTH, n_phys, PAGE_SIZE * B_HEADS * 2, HEAD_DIM)
    cb_v = jnp.transpose(cache_b_v.reshape(DEPTH, n_phys, PAGE_SIZE, B_HEADS, B_V_DIM // LANES, LANES),
                         (0, 1, 2, 4, 3, 5)).reshape(DEPTH, n_phys, PAGE_SIZE * B_HEADS * 2, LANES)
    ci_kt = jnp.swapaxes(cache_idx_k, 2, 3)
    tabs_p = _rope_tables(jnp.tile(jnp.arange(T), B))
    tabs_s = _rope_tables(jnp.tile(P + jnp.arange(Ts), DB))
    TS_PAD = GLA_CHUNK
    AKV = A_KV_HEADS * HEAD_DIM

    xp = x_prompt.reshape(B * T, D)
    xs = x_sample.reshape(DB * Ts, D)
    xp16, xs16 = xp.astype(BF16), xs.astype(BF16)
    outs = {k: [] for k in ('a_k_s', 'a_v_s', 'i_k_p', 'i_k_s', 'b_k_s', 'b_v_s', 'g_p', 'g_s')}
    cache_p = {'a_k': jnp.zeros((DEPTH * B * T * A_KV_HEADS, LANES), F32),
               'a_v': jnp.zeros((DEPTH * B * T * A_KV_HEADS, LANES), F32),
               'b_k': jnp.zeros((DEPTH * B * T * 2 * B_HEADS, LANES), F32),
               'b_v': jnp.zeros((DEPTH * B * T * 2 * B_HEADS, LANES), F32), 'depth': DEPTH}
    for l in range(DEPTH):
        lam_init = 0.8 - 0.6 * math.exp(-0.3 * l)
        wg = _split_w_in(w_in[l])
        lw = {'w_branch': w_branch[l].astype(BF16), 'w_out': w_out[l].astype(BF16), 'w_up': w_up[l],
              'w_down': w_down[l].astype(BF16), 'w_gate': wg['gate'], 'ln1_g': ln1_g[l], 'ln1_b': ln1_b[l],
              'ln2_g': ln2_g[l], 'ln2_b': ln2_b[l]}
        w2pad = jnp.zeros((LANES, C_HEADS * C_K_DIM), F32).at[MISC_GC:MISC_GC + GLA_RANK].set(gla_w2[l]).astype(BF16)
        lam_args = (lam_q1[l], lam_k1[l], lam_q2[l], lam_k2[l], diff_subln[l], lam_init)

        pr, cache_p = _project(xp16, wg, tabs_p, cache_p, l)
        r3 = lambda a: a.reshape(B, T, a.shape[-1])
        oA = dsa_prompt(r3(pr['qi']), r3(pr['misc']), r3(pr['kz']), r3(pr['q128']), r3(pr['k128']), r3(pr['v16']))
        oB = diff_prompt(r3(pr['q128']), r3(pr['k128']), r3(pr['v16']), *lam_args)
        oC, Sp = gla(r3(pr['c']), r3(pr['misc']), w2pad, gla_b[l], gla_norm[l],
                     jnp.zeros((B, C_HEADS, C_V_DIM, C_K_DIM), F32))
        xp, xp16 = _tail(xp, xp16, oA.reshape(B * T, -1), oB.reshape(B * T, -1), oC.reshape(B * T, -1), lw, alpha)
        outs['i_k_p'].append(pr['misc'][:, :IDX_DIM].reshape(B, T, IDX_DIM))
        outs['g_p'].append(jnp.swapaxes(Sp, -1, -2))

        ps, _ = _project(xs16, wg, tabs_s)
        ka_new = ps['k128f'][:, :AKV].reshape(DB, A_KV_HEADS, HEAD_DIM)
        va_new = ps['vf'][:, :AKV].reshape(DB, A_KV_HEADS, HEAD_DIM)
        kb_new = ps['k128f'][:, AKV:].reshape(DB, 2 * B_HEADS, HEAD_DIM)
        vb_new = ps['vf'][:, AKV:].reshape(DB, B_HEADS, B_V_DIM)
        ki_new = ps['misc'][:, :IDX_DIM]
        oA = dsa_sample(l, page_table,
                        ps['qi'].reshape(DB, IDX_HEADS, IDX_DIM),
                        ps['misc'][:, MISC_WI:MISC_WI + IDX_HEADS].reshape(DB, IDX_HEADS, 1),
                        ki_new.reshape(DB, 1, IDX_DIM),
                        ps['q128'][:, :A_HEADS * HEAD_DIM].reshape(DB, A_HEADS, HEAD_DIM),
                        jnp.repeat(ka_new, A_REP, axis=1), jnp.repeat(va_new, A_REP, axis=1),
                        ci_kt, ca_k, ca_v)
        oB = diff_sample(l, page_table, ps['q128'][:, A_HEADS * HEAD_DIM:].reshape(DB, 2 * B_HEADS, HEAD_DIM),
                         kb_new, jnp.repeat(vb_new, 2, axis=1), cb_k, cb_v, *lam_args)
        padt = lambda a: jnp.pad(a.reshape(DB, Ts, -1), ((0, 0), (0, TS_PAD - Ts), (0, 0)))
        oC, Ss = gla(padt(ps['c']), padt(ps['misc']), w2pad, gla_b[l], gla_norm[l],
                     jnp.swapaxes(state_gla[l], -1, -2), t_valid=Ts)
        xs, xs16 = _tail(xs, xs16, oA.reshape(DB, -1).astype(BF16), oB.reshape(DB, -1).astype(BF16),
                         oC[:, :Ts].reshape(DB * Ts, -1), lw, alpha)
        outs['a_k_s'].append(ka_new.reshape(DB, Ts, A_KV_HEADS, HEAD_DIM))
        outs['a_v_s'].append(va_new.reshape(DB, Ts, A_KV_HEADS, HEAD_DIM))
        outs['i_k_s'].append(ki_new.reshape(DB, Ts, IDX_DIM))
        outs['b_k_s'].append(kb_new.reshape(DB, Ts, B_HEADS, 2, HEAD_DIM))
        outs['b_v_s'].append(vb_new.reshape(DB, Ts, B_HEADS, B_V_DIM))
        outs['g_s'].append(jnp.swapaxes(Ss, -1, -2))

    st = {k: jnp.stack(v) for k, v in outs.items()}
    st['a_k_p'] = cache_p['a_k'].reshape(DEPTH, B, T, A_KV_HEADS, HEAD_DIM)
    st['a_v_p'] = cache_p['a_v'].reshape(DEPTH, B, T, A_KV_HEADS, HEAD_DIM)
    st['b_k_p'] = cache_p['b_k'].reshape(DEPTH, B, T, B_HEADS, 2, HEAD_DIM)
    st['b_v_p'] = jnp.transpose(cache_p['b_v'].reshape(DEPTH, B, T, B_V_DIM // LANES, B_HEADS, LANES),
                                (0, 1, 2, 4, 3, 5)).reshape(DEPTH, B, T, B_HEADS, B_V_DIM)
    return (xp.reshape(B, T, D), xs.reshape(DB, Ts, D),
            st['a_k_p'], st['a_k_s'], st['a_v_p'], st['a_v_s'], st['i_k_p'], st['i_k_s'],
            st['b_k_p'], st['b_k_s'], st['b_v_p'], st['b_v_s'], st['g_p'], st['g_s'])
```

```python
import functools
import math

import jax
import jax.numpy as jnp
from jax import lax
from jax.experimental import pallas as pl
from jax.experimental.pallas import tpu as pltpu

F32 = jnp.float32
BF16 = jnp.bfloat16
I32 = jnp.int32

LANES = 128
HEAD_DIM = 128
A_HEADS = 8
A_KV_HEADS = 2
A_REP = A_HEADS // A_KV_HEADS
IDX_HEADS = 16
IDX_DIM = 64
TOPK_MAX = 256
B_HEADS = 4
B_V_DIM = 256
C_HEADS = 4
C_V_DIM = 256
C_K_DIM = 128
GLA_RANK = 16
GLA_TAU = 16.0
GLA_CHUNK = 32
PAGE_SIZE = 128
Q_BLOCK = 128
ROPE_THETA = 10000.0
LN_EPS = 1e-5
RMS_EPS = 1e-6
BRANCH_WIDTH = 1024
D_MODEL = 2048

NEG = -1e30
INT_MIN = -(2 ** 31)
VMEM_LIMIT = 56 * 1024 * 1024

NT_DIMS = (((1,), (1,)), ((), ()))
TN_DIMS = (((0,), (0,)), ((), ()))

MISC_WI = IDX_DIM
MISC_GC = IDX_DIM + IDX_HEADS

SRC_SEGMENTS = (
    ('qa', 1024), ('ka', 256), ('va', 256), ('qi', 1024), ('ki', 64), ('wi', 16),
    ('qb', 1024), ('kb', 1024), ('vb', 1024), ('qc', 512), ('kc', 512), ('vc', 1024),
    ('gc', 16), ('rc', 1024), ('gate', 6144),
)


def _params(sem):
    return pltpu.CompilerParams(dimension_semantics=sem, vmem_limit_bytes=VMEM_LIMIT)


def _sortable(x):
    bits = lax.bitcast_convert_type(x, I32)
    return jnp.where(bits < 0, bits ^ jnp.int32(0x7FFFFFFF), bits)


def _swap_halves(blk, dh):
    if dh == LANES:
        return pltpu.roll(blk, LANES // 2, axis=1)
    lane = lax.broadcasted_iota(I32, blk.shape, 1)
    half = dh // 2
    return jnp.where(lane % dh < half, pltpu.roll(blk, LANES - half, axis=1), pltpu.roll(blk, half, axis=1))


def _proj_body(x_ref, w_ref, *rest, rope, scale, emit_kz):
    if rope:
        cos_ref, sin_ref = rest[:2]
        outs = rest[2:]
    else:
        outs = rest
    r = jnp.dot(x_ref[...], w_ref[...], preferred_element_type=F32)
    tn = r.shape[1]
    if emit_kz:
        o_ref, kz_ref = outs
        y = r * cos_ref[...] + _swap_halves(r, rope) * sin_ref[...]
        o_ref[...] = y
        lane = lax.broadcasted_iota(I32, y.shape, 1)
        kz0 = jnp.where(lane < IDX_DIM, y, 0.0)
        kz_ref[:, :LANES] = kz0.astype(BF16)
        kz_ref[:, LANES:] = pltpu.roll(kz0, IDX_DIM, axis=1).astype(BF16)
        return
    if rope:
        cos = cos_ref[...]
        sin = sin_ref[...]
        for g in range(tn // LANES):
            sl = slice(g * LANES, (g + 1) * LANES)
            blk = r[:, sl]
            y = blk * cos + _swap_halves(blk, rope) * sin
            if scale != 1.0:
                y = y * scale
            for o in outs:
                o[:, sl] = y.astype(o.dtype)
        return
    if scale != 1.0:
        r = r * scale
    for o in outs:
        o[...] = r.astype(o.dtype)


def proj(x, w, *, out_dtypes, rope=None, tables=None, scale=1.0, emit_kz=False, tm=1024, tn=1024, name='proj'):
    M, K = x.shape
    _, N = w.shape
    tm, tn = min(tm, M), min(tn, N)
    assert M % tm == 0 and N % tn == 0
    in_specs = [pl.BlockSpec((tm, K), lambda n, m: (m, 0)), pl.BlockSpec((K, tn), lambda n, m: (0, n))]
    args = [x, w]
    if rope:
        in_specs += [pl.BlockSpec((tm, LANES), lambda n, m: (m, 0))] * 2
        args += list(tables)
    out_specs = [pl.BlockSpec((tm, tn), lambda n, m: (m, n)) for _ in out_dtypes]
    out_shape = [jax.ShapeDtypeStruct((M, N), dt) for dt in out_dtypes]
    if emit_kz:
        out_specs.append(pl.BlockSpec((tm, 2 * LANES), lambda n, m: (m, 0)))
        out_shape.append(jax.ShapeDtypeStruct((M, 2 * LANES), BF16))
    return pl.pallas_call(
        functools.partial(_proj_body, rope=rope, scale=scale, emit_kz=emit_kz),
        grid=(N // tn, M // tm),
        in_specs=in_specs, out_specs=out_specs, out_shape=out_shape,
        compiler_params=_params(('parallel', 'parallel')),
        name=name,
    )(*args)


def _proj_cache_body(x_ref, w_ref, *rest, rope, b_rows):
    if rope:
        cos_ref, sin_ref = rest[:2]
        rest = rest[2:]
    ob_ref, oa_ref, og_ref = rest[-3:]
    tm = x_ref.shape[0]
    r = jnp.dot(x_ref[...], w_ref[...], preferred_element_type=F32)
    for g in range(r.shape[1] // LANES):
        sl = slice(g * LANES, (g + 1) * LANES)
        y = r[:, sl]
        if rope:
            y = y * cos_ref[...] + _swap_halves(y, rope) * sin_ref[...]
        ob_ref[:, sl] = y.astype(BF16)
        if g < A_KV_HEADS:
            oa_ref[pl.ds(g, tm, stride=A_KV_HEADS), :] = y
        else:
            og_ref[pl.ds(b_rows[g - A_KV_HEADS], tm, stride=len(b_rows)), :] = y


def proj_cache(x, w, bufs, layer, depth, *, b_rows, rope=None, tables=None, tm=1024, name='proj_cache'):
    M, K = x.shape
    _, N = w.shape
    tm = min(tm, M)
    nb = M // tm
    ng = len(b_rows)
    in_specs = [pl.BlockSpec((tm, K), lambda m: (m, 0)), pl.BlockSpec((K, N), lambda m: (0, 0))]
    args = [x, w]
    if rope:
        in_specs += [pl.BlockSpec((tm, LANES), lambda m: (m, 0))] * 2
        args += list(tables)
    aliases = {len(args): 1, len(args) + 1: 2}
    in_specs += [pl.BlockSpec(memory_space=pl.ANY)] * 2
    args += list(bufs)
    ob, buf_a, buf_g = pl.pallas_call(
        functools.partial(_proj_cache_body, rope=rope, b_rows=tuple(b_rows)),
        grid=(nb,),
        in_specs=in_specs,
        out_specs=[pl.BlockSpec((tm, N), lambda m: (m, 0)),
                   pl.BlockSpec((tm * A_KV_HEADS, LANES), lambda m: (layer * nb + m, 0)),
                   pl.BlockSpec((tm * ng, LANES), lambda m: (layer * nb + m, 0))],
        out_shape=[jax.ShapeDtypeStruct((M, N), BF16),
                   jax.ShapeDtypeStruct((depth * M * A_KV_HEADS, LANES), F32),
                   jax.ShapeDtypeStruct((depth * M * ng, LANES), F32)],
        input_output_aliases=aliases,
        compiler_params=_params(('arbitrary',)),
        name=name,
    )(*args)
    return ob, (buf_a, buf_g)


def _mm_act_body(x_ref, w_ref, o_ref):
    r = jnp.dot(x_ref[...], w_ref[...], preferred_element_type=F32)
    o_ref[...] = jnp.square(jnp.maximum(r, 0.0)).astype(o_ref.dtype)


def mlp_up(x, w, *, tm, tn):
    M, K = x.shape
    _, N = w.shape
    tm, tn = min(tm, M), min(tn, N)
    return pl.pallas_call(
        _mm_act_body,
        grid=(N // tn, M // tm),
        in_specs=[pl.BlockSpec((tm, K), lambda n, m: (m, 0)), pl.BlockSpec((K, tn), lambda n, m: (0, n))],
        out_specs=pl.BlockSpec((tm, tn), lambda n, m: (m, n)),
        out_shape=jax.ShapeDtypeStruct((M, N), BF16),
        compiler_params=_params(('parallel', 'parallel')),
        name='mlp_up',
    )(x, w)


def _mm_ln_body(x_ref, w_ref, r_ref, g_ref, b_ref, o_ref, ob_ref, acc_ref, *, nk, alpha):
    k = pl.program_id(1)

    @pl.when(k == 0)
    def _():
        acc_ref[...] = jnp.zeros_like(acc_ref)

    acc_ref[...] += jnp.dot(x_ref[...], w_ref[...], preferred_element_type=F32)

    @pl.when(k == nk - 1)
    def _():
        y = alpha * r_ref[...] + acc_ref[...]
        mu = jnp.mean(y, axis=-1, keepdims=True)
        yc = y - mu
        var = jnp.mean(yc * yc, axis=-1, keepdims=True)
        out = yc * lax.rsqrt(var + LN_EPS) * g_ref[...] + b_ref[...]
        o_ref[...] = out
        ob_ref[...] = out.astype(BF16)


def matmul_ln(x, w, resid, g, b, *, alpha, tm, tk, name='mm_ln'):
    M, K = x.shape
    _, N = w.shape
    tm, tk = min(tm, M), min(tk, K)
    nk = K // tk
    return pl.pallas_call(
        functools.partial(_mm_ln_body, nk=nk, alpha=alpha),
        grid=(M // tm, nk),
        in_specs=[pl.BlockSpec((tm, tk), lambda m, k: (m, k)),
                  pl.BlockSpec((tk, N), lambda m, k: (k, 0)),
                  pl.BlockSpec((tm, N), lambda m, k: (m, 0)),
                  pl.BlockSpec((1, N), lambda m, k: (0, 0)),
                  pl.BlockSpec((1, N), lambda m, k: (0, 0))],
        out_specs=[pl.BlockSpec((tm, N), lambda m, k: (m, 0)),
                   pl.BlockSpec((tm, N), lambda m, k: (m, 0))],
        out_shape=[jax.ShapeDtypeStruct((M, N), F32), jax.ShapeDtypeStruct((M, N), BF16)],
        scratch_shapes=[pltpu.VMEM((tm, N), F32)],
        compiler_params=_params(('parallel', 'arbitrary')),
        name=name,
    )(x, w, resid, g.reshape(1, N), b.reshape(1, N))


def _branch_body(x_ref, a_ref, b_ref, c_ref, wb_ref, wga_ref, wgb_ref, wgc_ref, o_ref):
    x = x_ref[...]
    acc = None
    for n, (br, wg) in enumerate(((a_ref, wga_ref), (b_ref, wgb_ref), (c_ref, wgc_ref))):
        gate = jax.nn.sigmoid(jnp.dot(x, wg[...], preferred_element_type=F32))
        t = gate * jnp.dot(br[...], wb_ref[n], preferred_element_type=F32)
        acc = t if acc is None else acc + t
    o_ref[...] = acc.astype(o_ref.dtype)


def branch_merge(x16, brA, brB, brC, w_branch, w_gate, *, tm, tn):
    M = x16.shape[0]
    tm = min(tm, M)
    nb = D_MODEL // tn
    gspecs = [pl.BlockSpec((D_MODEL, tn), functools.partial(lambda m, n, base: (0, base + n), base=i * nb))
              for i in range(3)]
    return pl.pallas_call(
        _branch_body,
        grid=(M // tm, nb),
        in_specs=[pl.BlockSpec((tm, D_MODEL), lambda m, n: (m, 0))]
        + [pl.BlockSpec((tm, BRANCH_WIDTH), lambda m, n: (m, 0))] * 3
        + [pl.BlockSpec((3, BRANCH_WIDTH, tn), lambda m, n: (0, 0, n))] + gspecs,
        out_specs=pl.BlockSpec((tm, tn), lambda m, n: (m, n)),
        out_shape=jax.ShapeDtypeStruct((M, D_MODEL), BF16),
        compiler_params=_params(('parallel', 'parallel')),
        name='branch_merge',
    )(x16, brA, brB, brC, w_branch, w_gate, w_gate, w_gate)


def _dsa_prompt_body(qi_ref, misc_ref, kz_ref, qa_ref, ka_ref, va_ref, o_ref, keys_ref, bias_ref,
                     *, top_k, tk, T, qrows):
    QB = qrows
    qb = pl.program_id(1)
    q0 = qb * QB
    nkt = (q0 + QB - 1) // tk + 1
    qpos = q0 + lax.broadcasted_iota(I32, (QB, tk), 0)
    lane = lax.broadcasted_iota(I32, (QB, tk), 1)
    qpos_c = q0 + lax.broadcasted_iota(I32, (QB, LANES), 0)
    lane_c = lax.broadcasted_iota(I32, (QB, LANES), 1)
    wi = misc_ref[:, MISC_WI:MISC_WI + IDX_HEADS]

    def score_tile(j, carry):
        off = pl.multiple_of(j * tk, tk)
        kz = (kz_ref[pl.ds(off, tk), :LANES], kz_ref[pl.ds(off, tk), LANES:])
        acc = jnp.zeros((QB, tk), F32)
        for h in range(IDX_HEADS):
            qp = qi_ref[:, (h // 2) * LANES:(h // 2 + 1) * LANES]
            s = lax.dot_general(qp, kz[h % 2], NT_DIMS, preferred_element_type=F32)
            acc = acc + wi[:, h:h + 1] * jnp.maximum(s, 0.0)
        key = jnp.where(off + lane <= qpos, _sortable(acc), INT_MIN)
        keys_ref[:, pl.ds(off, tk)] = key
        return carry

    lax.fori_loop(0, nkt, score_tile, 0)

    def count(pred):
        def body(j, acc):
            off = pl.multiple_of(j * tk, tk)
            for c in range(tk // LANES):
                kc = keys_ref[:, pl.ds(off + c * LANES, LANES)]
                acc = acc + jnp.where(pred(kc, off + c * LANES + lane_c), 1.0, 0.0)
            return acc
        acc = lax.fori_loop(0, nkt, body, jnp.zeros((QB, LANES), F32))
        return jnp.sum(acc, axis=1, keepdims=True)

    def bcast(v):
        return jnp.broadcast_to(v, (QB, LANES))

    kf = float(top_k)
    c0 = count(lambda kc, kp: kc >= 0)
    lo = jnp.where(c0 >= kf, jnp.int32(0), jnp.int32(INT_MIN))

    def bit_body(i, lo):
        cand = lo + (jnp.int32(1) << (30 - i))
        cand_b = bcast(cand)
        c = count(lambda kc, kp: kc >= cand_b)
        return jnp.where(c >= kf, cand, lo)

    thr = lax.fori_loop(0, 31, bit_body, lo)
    thr_b = bcast(thr)

    need = kf - count(lambda kc, kp: kc > thr_b)
    ceq = count(lambda kc, kp: kc == thr_b)
    nbits = max(T.bit_length(), 1)

    def tie_fn():
        def jb(i, j):
            cand = j + (jnp.int32(1) << (nbits - 1 - i))
            cand_b = bcast(cand)
            g = count(lambda kc, kp: (kc == thr_b) & (kp < cand_b))
            return jnp.where(g <= need, cand, j)
        return lax.fori_loop(0, nbits, jb, jnp.zeros((QB, 1), I32))

    jstar = lax.cond(jnp.max(ceq - need) > 0.0, tie_fn, lambda: jnp.full((QB, 1), 2 ** 30, I32))
    jstar_b = bcast(jstar)

    def bias_tile(j, carry):
        off = pl.multiple_of(j * tk, tk)
        for c in range(tk // LANES):
            kc = keys_ref[:, pl.ds(off + c * LANES, LANES)]
            kp = off + c * LANES + lane_c
            sel = ((kc > thr_b) | ((kc == thr_b) & (kp < jstar_b))) & (kp <= qpos_c)
            bias_ref[:, pl.ds(off + c * LANES, LANES)] = jnp.where(sel, 0.0, NEG)
        return carry

    lax.fori_loop(0, nkt, bias_tile, 0)

    AB = Q_BLOCK
    for sb in range(QB // AB):
        rs = slice(sb * AB, (sb + 1) * AB)
        nkt_s = (q0 + (sb + 1) * AB - 1) // tk + 1
        qgs = [jnp.concatenate([qa_ref[rs, (A_REP * g + r) * HEAD_DIM:(A_REP * g + r + 1) * HEAD_DIM]
                                for r in range(A_REP)], axis=0) for g in range(A_KV_HEADS)]

        def att_tile(j, carry, rs=rs, qgs=qgs):
            off = pl.multiple_of(j * tk, tk)
            b = bias_ref[rs, pl.ds(off, tk)]
            new = []
            for g in range(A_KV_HEADS):
                m, l, acc = carry[g]
                gs = slice(g * HEAD_DIM, (g + 1) * HEAD_DIM)
                kt = ka_ref[pl.ds(off, tk), gs]
                vt = va_ref[pl.ds(off, tk), gs]
                s = lax.dot_general(qgs[g], kt, NT_DIMS, preferred_element_type=F32)
                s = (s.reshape(A_REP, AB, tk) + b[None]).reshape(A_REP * AB, tk)
                m_new = jnp.maximum(m, jnp.max(s, axis=1, keepdims=True))
                corr = jnp.exp(m - m_new)
                p = jnp.exp(s - m_new)
                l = l * corr + jnp.sum(p, axis=1, keepdims=True)
                acc = acc * corr + jnp.dot(p.astype(BF16), vt, preferred_element_type=F32)
                new.append((m_new, l, acc))
            return tuple(new)

        one = (jnp.full((A_REP * AB, 1), NEG, F32), jnp.zeros((A_REP * AB, 1), F32),
               jnp.zeros((A_REP * AB, HEAD_DIM), F32))
        carry = lax.fori_loop(0, nkt_s // 2, lambda j2, c, f=att_tile: f(2 * j2 + 1, f(2 * j2, c)),
                              (one,) * A_KV_HEADS)
        carry = lax.fori_loop(2 * (nkt_s // 2), nkt_s, att_tile, carry)
        for g in range(A_KV_HEADS):
            _, l, acc = carry[g]
            out = acc / l
            for r in range(A_REP):
                h = A_REP * g + r
                o_ref[rs, h * HEAD_DIM:(h + 1) * HEAD_DIM] = out[r * AB:(r + 1) * AB].astype(o_ref.dtype)


def dsa_prompt(qi, misc, kz, q128, k128, v16):
    B, T, _ = qi.shape
    top_k = min(TOPK_MAX, T // 4)
    tk = min(512, T)
    qrows = min(Q_BLOCK, T)
    nb = T // qrows
    kvw = A_KV_HEADS * HEAD_DIM
    return pl.pallas_call(
        functools.partial(_dsa_prompt_body, top_k=top_k, tk=tk, T=T, qrows=qrows),
        grid=(B, nb),
        in_specs=[pl.BlockSpec((None, qrows, IDX_HEADS * IDX_DIM), lambda b, q: (b, q, 0)),
                  pl.BlockSpec((None, qrows, LANES), lambda b, q: (b, q, 0)),
                  pl.BlockSpec((None, T, 2 * LANES), lambda b, q: (b, 0, 0)),
                  pl.BlockSpec((None, qrows, A_HEADS * HEAD_DIM), lambda b, q: (b, q, 0)),
                  pl.BlockSpec((None, T, kvw), lambda b, q: (b, 0, 0)),
                  pl.BlockSpec((None, T, kvw), lambda b, q: (b, 0, 0))],
        out_specs=pl.BlockSpec((None, qrows, A_HEADS * HEAD_DIM), lambda b, q: (b, q, 0)),
        out_shape=jax.ShapeDtypeStruct((B, T, A_HEADS * HEAD_DIM), BF16),
        scratch_shapes=[pltpu.VMEM((qrows, T), I32), pltpu.VMEM((qrows, T), F32)],
        compiler_params=_params(('parallel', 'arbitrary')),
        name='dsa_prompt',
    )(qi, misc, kz, q128, k128, v16)


def _lam(lq1, lk1, lq2, lk2, lam_init):
    return (jnp.exp(jnp.sum(lq1[...] * lk1[...], axis=-1, keepdims=True))
            - jnp.exp(jnp.sum(lq2[...] * lk2[...], axis=-1, keepdims=True)) + lam_init)


def _diff_finish(o0, o1, lam, subln, lam_init):
    a = o0 - lam * o1
    a = a * lax.rsqrt(jnp.mean(a * a, axis=-1, keepdims=True) + RMS_EPS) * subln
    return a * (1.0 - lam_init)


def _diff_prompt_body(lq1, lk1, lq2, lk2, sub_ref, q_ref, k_ref, v_ref, o_ref, *, tq, tk, lam_init):
    qi = pl.program_id(2)
    q0 = qi * tq
    n_full = q0 // tk
    n_all = (q0 + tq - 1) // tk + 1
    qpos = q0 + lax.broadcasted_iota(I32, (tq, tk), 0)
    lane = lax.broadcasted_iota(I32, (tq, tk), 1)
    qs = [q_ref[:, mp * HEAD_DIM:(mp + 1) * HEAD_DIM] for mp in range(2)]

    def tile(j, carry, masked):
        off = pl.multiple_of(j * tk, tk)
        vt = v_ref[pl.ds(off, tk), :]
        new = []
        for mp in range(2):
            m, l, acc = carry[mp]
            kt = k_ref[pl.ds(off, tk), mp * HEAD_DIM:(mp + 1) * HEAD_DIM]
            s = lax.dot_general(qs[mp], kt, NT_DIMS, preferred_element_type=F32)
            if masked:
                s = jnp.where(off + lane <= qpos, s, NEG)
            m_new = jnp.maximum(m, jnp.max(s, axis=1, keepdims=True))
            corr = jnp.exp(m - m_new)
            p = jnp.exp(s - m_new)
            l = l * corr + jnp.sum(p, axis=1, keepdims=True)
            acc = acc * corr + jnp.dot(p.astype(BF16), vt, preferred_element_type=F32)
            new.append((m_new, l, acc))
        return tuple(new)

    one = (jnp.full((tq, 1), NEG, F32), jnp.zeros((tq, 1), F32), jnp.zeros((tq, B_V_DIM), F32))
    def tile2(j2, carry):
        return tile(2 * j2 + 1, tile(2 * j2, carry, masked=False), masked=False)

    carry = lax.fori_loop(0, n_full // 2, tile2, (one, one))
    carry = lax.fori_loop(2 * (n_full // 2), n_full, functools.partial(tile, masked=False), carry)
    carry = lax.fori_loop(n_full, n_all, functools.partial(tile, masked=True), carry)
    outs = [acc / l for (_, l, acc) in carry]
    lam = _lam(lq1, lk1, lq2, lk2, lam_init)
    o_ref[...] = _diff_finish(outs[0], outs[1], lam, sub_ref[...], lam_init).astype(o_ref.dtype)


def diff_prompt(q128, k128, v16, lam_q1, lam_k1, lam_q2, lam_k2, subln, lam_init):
    B, T, _ = q128.shape
    H = B_HEADS
    tq = min(512, T)
    tk = min(512, T)
    pw = 2 * HEAD_DIM
    qb0 = (A_HEADS * HEAD_DIM) // pw
    kb0 = (A_KV_HEADS * HEAD_DIM) // pw
    vec = pl.BlockSpec((1, HEAD_DIM), lambda b, h, q: (0, 0))
    return pl.pallas_call(
        functools.partial(_diff_prompt_body, tq=tq, tk=tk, lam_init=lam_init),
        grid=(B, H, T // tq),
        in_specs=[vec, vec, vec, vec,
                  pl.BlockSpec((1, B_V_DIM), lambda b, h, q: (0, 0)),
                  pl.BlockSpec((None, tq, pw), lambda b, h, q: (b, q, qb0 + h)),
                  pl.BlockSpec((None, T, pw), lambda b, h, q: (b, 0, kb0 + h)),
                  pl.BlockSpec((None, T, B_V_DIM), lambda b, h, q: (b, 0, kb0 + h))],
        out_specs=pl.BlockSpec((None, tq, B_V_DIM), lambda b, h, q: (b, q, h)),
        out_shape=jax.ShapeDtypeStruct((B, T, H * B_V_DIM), BF16),
        compiler_params=_params(('parallel', 'parallel', 'arbitrary')),
        name='diff_prompt',
    )(lam_q1.reshape(1, -1), lam_k1.reshape(1, -1), lam_q2.reshape(1, -1), lam_k2.reshape(1, -1),
      subln.reshape(1, -1), q128, k128, v16)


def _gla_body(q_ref, k_ref, v_ref, rc_ref, misc_ref, w2_ref, gb_ref, gn_ref, s0_ref, o_ref, sfin_ref, st_ref,
              *, tb, chunk, t_valid, nt):
    t = pl.program_id(1)

    @pl.when(t == 0)
    def _():
        st_ref[...] = s0_ref[...]

    x = jnp.dot(misc_ref[...].astype(BF16), w2_ref[...], preferred_element_type=F32) + gb_ref[...]
    la = (jnp.minimum(x, 0.0) - jnp.log(1.0 + jnp.exp(-jnp.abs(x)))) * (1.0 / GLA_TAU)
    W = C_HEADS * C_K_DIM
    row = lax.broadcasted_iota(I32, (tb, W), 0)
    if t_valid is not None:
        la = jnp.where(t * tb + row < t_valid, la, 0.0)
    rowc = row % chunk
    b = la
    sh = 1
    while sh < chunk:
        b = b + jnp.where(rowc >= sh, pltpu.roll(b, sh, axis=0), 0.0)
        sh *= 2
    k = k_ref[...]
    qe = (q_ref[...] * (C_K_DIM ** -0.5) * jnp.exp(b)).astype(BF16)
    ke = (k * jnp.exp(-b)).astype(BF16)
    v16 = v_ref[...].astype(BF16)
    rc = rc_ref[...]
    gn = gn_ref[...]
    tril = lax.broadcasted_iota(I32, (chunk, chunk), 0) >= lax.broadcasted_iota(I32, (chunk, chunk), 1)
    for c in range(tb // chunk):
        r0 = c * chunk
        bl = b[r0 + chunk - 1:r0 + chunk, :]
        kd = (k[r0:r0 + chunk] * jnp.exp(bl - b[r0:r0 + chunk])).astype(BF16)
        dec = jnp.exp(bl)
        for h in range(C_HEADS):
            ck = slice(h * C_K_DIM, (h + 1) * C_K_DIM)
            cv = slice(h * C_V_DIM, (h + 1) * C_V_DIM)
            qe_c = qe[r0:r0 + chunk, ck]
            v_c = v16[r0:r0 + chunk, cv]
            att = lax.dot_general(qe_c, ke[r0:r0 + chunk, ck], NT_DIMS, preferred_element_type=F32)
            att = jnp.where(tril, att, 0.0).astype(BF16)
            st = st_ref[h]
            o_c = (lax.dot_general(qe_c, st.astype(BF16), NT_DIMS, preferred_element_type=F32)
                   + jnp.dot(att, v_c, preferred_element_type=F32))
            st_ref[h] = st * dec[:, ck] + lax.dot_general(v_c, kd[:, ck], TN_DIMS, preferred_element_type=F32)
            o_n = o_c * lax.rsqrt(jnp.mean(o_c * o_c, axis=-1, keepdims=True) + RMS_EPS) * gn
            r_c = rc[r0:r0 + chunk, cv]
            o_ref[r0:r0 + chunk, cv] = (o_n * (r_c * jax.nn.sigmoid(r_c))).astype(o_ref.dtype)

    @pl.when(t == nt - 1)
    def _():
        sfin_ref[...] = st_ref[...]


def gla(cproj, misc, w2pad, gb, gn, s0t, *, t_valid=None):
    B, T, _ = cproj.shape
    chunk = GLA_CHUNK
    tb = min(256, T)
    nt = T // tb
    W = C_HEADS * C_K_DIM
    V = C_HEADS * C_V_DIM
    st_spec = pl.BlockSpec((None, C_HEADS, C_V_DIM, C_K_DIM), lambda b, t: (b, 0, 0, 0))
    return pl.pallas_call(
        functools.partial(_gla_body, tb=tb, chunk=chunk, t_valid=t_valid, nt=nt),
        grid=(B, nt),
        in_specs=[pl.BlockSpec((None, tb, W), lambda b, t: (b, t, 0)),
                  pl.BlockSpec((None, tb, W), lambda b, t: (b, t, 1)),
                  pl.BlockSpec((None, tb, V), lambda b, t: (b, t, 1)),
                  pl.BlockSpec((None, tb, V), lambda b, t: (b, t, 2)),
                  pl.BlockSpec((None, tb, LANES), lambda b, t: (b, t, 0)),
                  pl.BlockSpec((LANES, W), lambda b, t: (0, 0)),
                  pl.BlockSpec((1, W), lambda b, t: (0, 0)),
                  pl.BlockSpec((1, C_V_DIM), lambda b, t: (0, 0)),
                  st_spec],
        out_specs=[pl.BlockSpec((None, tb, V), lambda b, t: (b, t, 0)), st_spec],
        out_shape=[jax.ShapeDtypeStruct((B, T, V), BF16),
                   jax.ShapeDtypeStruct((B, C_HEADS, C_V_DIM, C_K_DIM), F32)],
        scratch_shapes=[pltpu.VMEM((C_HEADS, C_V_DIM, C_K_DIM), F32)],
        compiler_params=_params(('parallel', 'arbitrary')),
        name='gla',
    )(cproj, cproj, cproj, cproj, misc, w2pad, gb.reshape(1, W), gn.reshape(1, C_V_DIM), s0t)


def _page_specs(block, layer, pps):
    def mk(i):
        return pl.BlockSpec((None, None) + block,
                            lambda b, p, pt: (layer, pt[b, p * pps + i]) + (0,) * len(block))
    return [mk(i) for i in range(pps)]


def _dsa_sample_score_body(pt_ref, qi_ref, wi_ref, kin_ref, *rest, pps, n_pages):
    pages, o_ref = rest[:pps], rest[pps]
    p = pl.program_id(1)
    qi = qi_ref[...]
    wi = wi_ref[...]

    @pl.when(p == 0)
    def _():
        o_ref[...] = jnp.full(o_ref.shape, -jnp.inf, F32)
        s = jnp.sum(qi.astype(F32) * kin_ref[...], axis=1, keepdims=True)
        snew = jnp.sum(wi * jnp.maximum(s, 0.0), axis=0, keepdims=True)
        lane = lax.broadcasted_iota(I32, (1, PAGE_SIZE), 1)
        o_ref[n_pages:n_pages + 1, :] = jnp.where(lane == 0, snew, -jnp.inf)

    kpt = jnp.concatenate([pg[...].astype(BF16) for pg in pages], axis=1)
    s = jnp.dot(qi, kpt, preferred_element_type=F32)
    sc = jnp.sum(wi * jnp.maximum(s, 0.0), axis=0, keepdims=True)
    for i in range(pps):
        o_ref[pl.ds(p * pps + i, 1), :] = sc[:, i * PAGE_SIZE:(i + 1) * PAGE_SIZE]


def _topk_bias(sc, top_k, n_valid):
    R = sc.shape[0]
    key = _sortable(sc)
    pos = lax.broadcasted_iota(I32, sc.shape, 0) * PAGE_SIZE + lax.broadcasted_iota(I32, sc.shape, 1)
    key = jnp.where(pos < n_valid, key, INT_MIN)

    def count(hit):
        c = jnp.sum(jnp.where(hit, 1.0, 0.0), axis=1, keepdims=True)
        return jnp.sum(c, axis=0, keepdims=True)

    kf = float(top_k)
    lo = jnp.where(count(key >= 0) >= kf, jnp.int32(0), jnp.int32(INT_MIN))

    def bit_body(i, lo):
        cand = lo + (jnp.int32(1) << (30 - i))
        return jnp.where(count(key >= cand) >= kf, cand, lo)

    thr = lax.fori_loop(0, 31, bit_body, lo)
    need = kf - count(key > thr)
    nbits = (R * PAGE_SIZE).bit_length()

    def jb(i, j):
        cand = j + (jnp.int32(1) << (nbits - 1 - i))
        return jnp.where(count((key == thr) & (pos < cand)) <= need, cand, j)

    jstar = lax.fori_loop(0, nbits, jb, jnp.zeros((1, 1), I32))
    sel = ((key > thr) | ((key == thr) & (pos < jstar))) & (pos < n_valid)
    return jnp.where(sel, 0.0, NEG)


def _dsa_sample_attn_body(pt_ref, sc_ref, q_ref, kn_ref, vn_ref, *rest, pps, n_pages, top_k):
    kpages, vpages = rest[:pps], rest[pps:2 * pps]
    o_ref, bias_ref, m_ref, l_ref, acc_ref = rest[2 * pps:]
    p = pl.program_id(1)
    q = q_ref[...]
    first = lax.broadcasted_iota(I32, (A_HEADS, PAGE_SIZE), 0) < A_REP

    @pl.when(p == 0)
    def _():
        bias_ref[...] = _topk_bias(sc_ref[...], top_k, n_pages * PAGE_SIZE + 1)
        m_ref[...] = jnp.full(m_ref.shape, NEG, F32)
        l_ref[...] = jnp.zeros_like(l_ref)
        acc_ref[...] = jnp.zeros_like(acc_ref)

    def kv(refs, g):
        return jnp.concatenate([r[pl.ds(g, PAGE_SIZE, stride=A_KV_HEADS), :].astype(BF16) for r in refs], axis=0)

    W = pps * PAGE_SIZE
    first_w = lax.broadcasted_iota(I32, (A_HEADS, W), 0) < A_REP
    s0 = lax.dot_general(q, kv(kpages, 0), NT_DIMS, preferred_element_type=F32)
    s1 = lax.dot_general(q, kv(kpages, 1), NT_DIMS, preferred_element_type=F32)
    bias = jnp.concatenate([bias_ref[pl.ds(p * pps + i, 1), :] for i in range(pps)], axis=1)
    s = jnp.where(first_w, s0, s1) + bias
    m = m_ref[...]
    m_new = jnp.maximum(m, jnp.max(s, axis=1, keepdims=True))
    corr = jnp.exp(m - m_new)
    pr = jnp.exp(s - m_new)
    l_ref[...] = l_ref[...] * corr + jnp.sum(pr, axis=1, keepdims=True)
    pb = pr.astype(BF16)
    pv = jnp.where(first, jnp.dot(pb, kv(vpages, 0), preferred_element_type=F32),
                   jnp.dot(pb, kv(vpages, 1), preferred_element_type=F32))
    acc_ref[...] = acc_ref[...] * corr + pv
    m_ref[...] = m_new

    @pl.when(p == pl.num_programs(1) - 1)
    def _():
        s = (jnp.sum(q.astype(F32) * kn_ref[...], axis=1, keepdims=True)
             + bias_ref[n_pages:n_pages + 1, 0:1])
        m = m_ref[...]
        m_new = jnp.maximum(m, s)
        corr = jnp.exp(m - m_new)
        pr = jnp.exp(s - m_new)
        l = l_ref[...] * corr + pr
        o_ref[...] = (acc_ref[...] * corr + pr * vn_ref[...]) / l


def dsa_sample(layer, page_table, qi, wi, ki_new, qa, ka_new, va_new, cache_idx_kt, cache_a_k, cache_a_v):
    DB, n_pages = page_table.shape
    pps_s = math.gcd(32, n_pages)
    pps = math.gcd(16, n_pages)
    R = ((n_pages + 1 + 7) // 8) * 8
    L = n_pages * PAGE_SIZE + 1
    top_k = min(TOPK_MAX, L // 4)
    per_b = lambda *blk: pl.BlockSpec((None,) + blk, lambda b, p, pt: (b,) + (0,) * len(blk))
    scores = pl.pallas_call(
        functools.partial(_dsa_sample_score_body, pps=pps_s, n_pages=n_pages),
        grid_spec=pltpu.PrefetchScalarGridSpec(
            num_scalar_prefetch=1, grid=(DB, n_pages // pps_s),
            in_specs=[per_b(IDX_HEADS, IDX_DIM), per_b(IDX_HEADS, 1), per_b(1, IDX_DIM)]
            + _page_specs((IDX_DIM, PAGE_SIZE), layer, pps_s),
            out_specs=per_b(R, PAGE_SIZE)),
        out_shape=jax.ShapeDtypeStruct((DB, R, PAGE_SIZE), F32),
        compiler_params=_params(('parallel', 'arbitrary')),
        name='dsa_sample_scores',
    )(page_table, qi, wi, ki_new, *([cache_idx_kt] * pps_s))
    rows = PAGE_SIZE * A_KV_HEADS
    return pl.pallas_call(
        functools.partial(_dsa_sample_attn_body, pps=pps, n_pages=n_pages, top_k=top_k),
        grid_spec=pltpu.PrefetchScalarGridSpec(
            num_scalar_prefetch=1, grid=(DB, n_pages // pps),
            in_specs=[per_b(R, PAGE_SIZE), per_b(A_HEADS, HEAD_DIM), per_b(A_HEADS, HEAD_DIM), per_b(A_HEADS, HEAD_DIM)]
            + _page_specs((rows, HEAD_DIM), layer, pps) + _page_specs((rows, HEAD_DIM), layer, pps),
            out_specs=per_b(A_HEADS, HEAD_DIM),
            scratch_shapes=[pltpu.VMEM((R, PAGE_SIZE), F32), pltpu.VMEM((A_HEADS, 1), F32),
                            pltpu.VMEM((A_HEADS, 1), F32), pltpu.VMEM((A_HEADS, HEAD_DIM), F32)]),
        out_shape=jax.ShapeDtypeStruct((DB, A_HEADS, HEAD_DIM), F32),
        compiler_params=_params(('parallel', 'arbitrary')),
        name='dsa_sample_attn',
    )(page_table, scores, qa, ka_new, va_new, *([cache_a_k] * pps), *([cache_a_v] * pps))


def _diff_sample_body(pt_ref, lq1, lk1, lq2, lk2, sub_ref, q_ref, kn_ref, vn_ref, *rest, pps, lam_init):
    kpages, vpages = rest[:pps], rest[pps:2 * pps]
    o_ref, m_ref, l_ref, acc_ref = rest[2 * pps:]
    p = pl.program_id(1)
    NJ = 2 * B_HEADS
    q = q_ref[...]
    rowk = lax.broadcasted_iota(I32, (NJ, PAGE_SIZE), 0)

    @pl.when(p == 0)
    def _():
        m_ref[...] = jnp.full(m_ref.shape, NEG, F32)
        l_ref[...] = jnp.zeros_like(l_ref)
        acc_ref[...] = jnp.zeros_like(acc_ref)

    def rows(refs, j):
        return jnp.concatenate([r[pl.ds(j, PAGE_SIZE, stride=NJ), :].astype(BF16) for r in refs], axis=0)

    W = pps * PAGE_SIZE
    roww = lax.broadcasted_iota(I32, (NJ, W), 0)
    s = jnp.zeros((NJ, W), F32)
    for j in range(NJ):
        sj = lax.dot_general(q, rows(kpages, j), NT_DIMS, preferred_element_type=F32)
        s = jnp.where(roww == j, sj, s)
    m = m_ref[...]
    m_new = jnp.maximum(m, jnp.max(s, axis=1, keepdims=True))
    corr = jnp.exp(m - m_new)
    pr = jnp.exp(s - m_new)
    l_ref[...] = l_ref[...] * corr + jnp.sum(pr, axis=1, keepdims=True)
    pb = pr.astype(BF16)
    halves = []
    for c in range(B_V_DIM // LANES):
        pv = jnp.zeros((NJ, LANES), F32)
        for h in range(B_HEADS):
            ph = jnp.dot(pb, rows(vpages, c * B_HEADS + h), preferred_element_type=F32)
            pv = jnp.where(rowk // 2 == h, ph, pv)
        halves.append(pv)
    acc_ref[...] = acc_ref[...] * corr + jnp.concatenate(halves, axis=1)
    m_ref[...] = m_new

    @pl.when(p == pl.num_programs(1) - 1)
    def _():
        s = jnp.sum(q.astype(F32) * kn_ref[...], axis=1, keepdims=True)
        m = m_ref[...]
        m_new = jnp.maximum(m, s)
        corr = jnp.exp(m - m_new)
        pr = jnp.exp(s - m_new)
        l = l_ref[...] * corr + pr
        o = (acc_ref[...] * corr + pr * vn_ref[...]) / l
        lam = _lam(lq1, lk1, lq2, lk2, lam_init)
        for h in range(B_HEADS):
            o_ref[h:h + 1, :] = _diff_finish(o[2 * h:2 * h + 1], o[2 * h + 1:2 * h + 2], lam, sub_ref[...], lam_init)


def diff_sample(layer, page_table, qb, kb_new, vb_new, cache_b_k, cache_b_v,
                lam_q1, lam_k1, lam_q2, lam_k2, subln, lam_init):
    DB, n_pages = page_table.shape
    pps = math.gcd(8, n_pages)
    NJ = 2 * B_HEADS
    per_b = lambda *blk: pl.BlockSpec((None,) + blk, lambda b, p, pt: (b,) + (0,) * len(blk))
    vec = lambda w: pl.BlockSpec((1, w), lambda b, p, pt: (0, 0))
    return pl.pallas_call(
        functools.partial(_diff_sample_body, pps=pps, lam_init=lam_init),
        grid_spec=pltpu.PrefetchScalarGridSpec(
            num_scalar_prefetch=1, grid=(DB, n_pages // pps),
            in_specs=[vec(HEAD_DIM)] * 4 + [vec(B_V_DIM),
                      per_b(NJ, HEAD_DIM), per_b(NJ, HEAD_DIM), per_b(NJ, B_V_DIM)]
            + _page_specs((PAGE_SIZE * NJ, HEAD_DIM), layer, pps)
            + _page_specs((PAGE_SIZE * NJ, LANES), layer, pps),
            out_specs=per_b(B_HEADS, B_V_DIM),
            scratch_shapes=[pltpu.VMEM((NJ, 1), F32), pltpu.VMEM((NJ, 1), F32), pltpu.VMEM((NJ, B_V_DIM), F32)]),
        out_shape=jax.ShapeDtypeStruct((DB, B_HEADS, B_V_DIM), F32),
        compiler_params=_params(('parallel', 'arbitrary')),
        name='diff_sample',
    )(page_table, lam_q1.reshape(1, -1), lam_k1.reshape(1, -1), lam_q2.reshape(1, -1), lam_k2.reshape(1, -1),
      subln.reshape(1, -1), qb, kb_new, vb_new, *([cache_b_k] * pps), *([cache_b_v] * pps))


def _split_w_in(w):
    src = {}
    off = 0
    for name, width in SRC_SEGMENTS:
        src[name] = w[:, off:off + width]
        off += width
    cat = lambda *names: jnp.concatenate([src[n] for n in names], axis=1).astype(BF16)
    pad = jnp.zeros((w.shape[0], LANES - IDX_DIM - IDX_HEADS - GLA_RANK), w.dtype)
    return {
        'q128': cat('qa', 'qb'), 'k128': cat('ka', 'kb'), 'qi': cat('qi'), 'v': cat('va', 'vb'),
        'c': cat('qc', 'kc', 'vc', 'rc'), 'gate': cat('gate'),
        'misc': jnp.concatenate([src['ki'], src['wi'], src['gc'], pad], axis=1).astype(BF16),
    }


def _rope_tables(pos):
    def tab(dh):
        half = dh // 2
        inv_freq = ROPE_THETA ** (-jnp.arange(half, dtype=F32) / half)
        ang = pos.astype(F32)[:, None] * inv_freq[None, :]
        c, s = jnp.cos(ang), jnp.sin(ang)
        reps = LANES // dh
        return jnp.tile(jnp.concatenate([c, c], axis=1), (1, reps)), jnp.tile(jnp.concatenate([-s, s], axis=1), (1, reps))
    c128, s128 = tab(HEAD_DIM)
    c64, s64 = tab(IDX_DIM)
    n = pos.shape[0]
    tail_c = jnp.concatenate([jnp.full((n, IDX_HEADS), IDX_HEADS ** -0.5, F32),
                              jnp.ones((n, LANES - IDX_DIM - IDX_HEADS), F32)], axis=1)
    cm = jnp.concatenate([c64[:, :IDX_DIM], tail_c], axis=1)
    sm = jnp.concatenate([s64[:, :IDX_DIM], jnp.zeros((n, LANES - IDX_DIM), F32)], axis=1)
    return (c128, s128), (c64, s64), (cm, sm)


BK_ROWS = tuple(range(2 * B_HEADS))
BV_ROWS = tuple((g % 2) * B_HEADS + g // 2 for g in range(2 * B_HEADS))


def _project(x16, wg, tabs, cache=None, layer=0):
    t128, t64, tm_ = tabs
    q128, = proj(x16, wg['q128'], out_dtypes=(BF16,), rope=HEAD_DIM, tables=t128, scale=HEAD_DIM ** -0.5, name='proj_q128')
    qi, = proj(x16, wg['qi'], out_dtypes=(BF16,), rope=IDX_DIM, tables=t64, scale=IDX_DIM ** -0.5, name='proj_qi')
    misc, kz = proj(x16, wg['misc'], out_dtypes=(F32,), rope=IDX_DIM, tables=tm_, emit_kz=True, name='proj_misc')
    cproj, = proj(x16, wg['c'], out_dtypes=(F32,), name='proj_c')
    out = dict(q128=q128, qi=qi, misc=misc, kz=kz, c=cproj)
    if cache is None:
        out['k128f'], out['k128'] = proj(x16, wg['k128'], out_dtypes=(F32, BF16), rope=HEAD_DIM, tables=t128, tn=1280,
                                         name='proj_k128')
        out['vf'], out['v16'] = proj(x16, wg['v'], out_dtypes=(F32, BF16), tn=1280, name='proj_v')
        return out, None
    depth = cache['depth']
    out['k128'], (ak, bk) = proj_cache(x16, wg['k128'], (cache['a_k'], cache['b_k']), layer, depth, b_rows=BK_ROWS,
                                       rope=HEAD_DIM, tables=t128, name='proj_k128c')
    out['v16'], (av, bv) = proj_cache(x16, wg['v'], (cache['a_v'], cache['b_v']), layer, depth, b_rows=BV_ROWS,
                                      name='proj_vc')
    return out, dict(a_k=ak, b_k=bk, a_v=av, b_v=bv, depth=depth)


def _tail(x, x16, oA, oB, oC, lw, alpha):
    g = branch_merge(x16, oA, oB, oC, lw['w_branch'], lw['w_gate'], tm=1024, tn=512)
    x1, x1b = matmul_ln(g, lw['w_out'], x, lw['ln1_g'], lw['ln1_b'], alpha=alpha, tm=512, tk=2048, name='out_ln1')
    hid = mlp_up(x1b, lw['w_up'], tm=1024, tn=1024)
    return matmul_ln(hid, lw['w_down'], x1, lw['ln2_g'], lw['ln2_b'], alpha=alpha, tm=512, tk=2048, name='down_ln2')


def kernel(x_prompt, x_sample, cache_a_k, cache_a_v, cache_idx_k, cache_b_k, cache_b_v, state_gla, page_table,
           w_in, gla_w2, gla_b, lam_q1, lam_k1, lam_q2, lam_k2, diff_subln, gla_norm, w_branch, w_out,
           ln1_g, ln1_b, w_up, w_down, ln2_g, ln2_b):
    B, T, D = x_prompt.shape
    DB, Ts, _ = x_sample.shape
    assert Ts == 1
    DEPTH = w_in.shape[0]
    n_pages = page_table.shape[1]
    P = n_pages * PAGE_SIZE
    alpha = (2.0 * DEPTH) ** 0.25
    n_phys = cache_a_k.shape[1]
    ca_k = cache_a_k.reshape(DEPTH, n_phys, PAGE_SIZE * A_KV_HEADS, HEAD_DIM)
    ca_v = cache_a_v.reshape(DEPTH, n_phys, PAGE_SIZE * A_KV_HEADS, HEAD_DIM)
    cb_k = cache_b_k.reshape(DEPTH, n_phys, PAGE_SIZE * B_HEADS * 2, HEAD_DIM)
    cb_v = jnp.transpose(cache_b_v.reshape(DEPTH, n_phys, PAGE_SIZE, B_HEADS, B_V_DIM // LANES, LANES),
                         (0, 1, 2, 4, 3, 5)).reshape(DEPTH, n_phys, PAGE_SIZE * B_HEADS * 2, LANES)
    ci_kt = jnp.swapaxes(cache_idx_k, 2, 3)
    tabs_p = _rope_tables(jnp.tile(jnp.arange(T), B))
    tabs_s = _rope_tables(jnp.tile(P + jnp.arange(Ts), DB))
    TS_PAD = GLA_CHUNK
    AKV = A_KV_HEADS * HEAD_DIM

    xp = x_prompt.reshape(B * T, D)
    xs = x_sample.reshape(DB * Ts, D)
    xp16, xs16 = xp.astype(BF16), xs.astype(BF16)
    outs = {k: [] for k in ('a_k_s', 'a_v_s', 'i_k_p', 'i_k_s', 'b_k_s', 'b_v_s', 'g_p', 'g_s')}
    cache_p = {'a_k': jnp.zeros((DEPTH * B * T * A_KV_HEADS, LANES), F32),
               'a_v': jnp.zeros((DEPTH * B * T * A_KV_HEADS, LANES), F32),
               'b_k': jnp.zeros((DEPTH * B * T * 2 * B_HEADS, LANES), F32),
               'b_v': jnp.zeros((DEPTH * B * T * 2 * B_HEADS, LANES), F32), 'depth': DEPTH}
    for l in range(DEPTH):
        lam_init = 0.8 - 0.6 * math.exp(-0.3 * l)
        wg = _split_w_in(w_in[l])
        lw = {'w_branch': w_branch[l].astype(BF16), 'w_out': w_out[l].astype(BF16), 'w_up': w_up[l].astype(BF16),
              'w_down': w_down[l].astype(BF16), 'w_gate': wg['gate'], 'ln1_g': ln1_g[l], 'ln1_b': ln1_b[l],
              'ln2_g': ln2_g[l], 'ln2_b': ln2_b[l]}
        w2pad = jnp.zeros((LANES, C_HEADS * C_K_DIM), F32).at[MISC_GC:MISC_GC + GLA_RANK].set(gla_w2[l]).astype(BF16)
        lam_args = (lam_q1[l], lam_k1[l], lam_q2[l], lam_k2[l], diff_subln[l], lam_init)

        pr, cache_p = _project(xp16, wg, tabs_p, cache_p, l)
        r3 = lambda a: a.reshape(B, T, a.shape[-1])
        oA = dsa_prompt(r3(pr['qi']), r3(pr['misc']), r3(pr['kz']), r3(pr['q128']), r3(pr['k128']), r3(pr['v16']))
        oB = diff_prompt(r3(pr['q128']), r3(pr['k128']), r3(pr['v16']), *lam_args)
        oC, Sp = gla(r3(pr['c']), r3(pr['misc']), w2pad, gla_b[l], gla_norm[l],
                     jnp.zeros((B, C_HEADS, C_V_DIM, C_K_DIM), F32))
        xp, xp16 = _tail(xp, xp16, oA.reshape(B * T, -1), oB.reshape(B * T, -1), oC.reshape(B * T, -1), lw, alpha)
        outs['i_k_p'].append(pr['misc'][:, :IDX_DIM].reshape(B, T, IDX_DIM))
        outs['g_p'].append(jnp.swapaxes(Sp, -1, -2))

        ps, _ = _project(xs16, wg, tabs_s)
        ka_new = ps['k128f'][:, :AKV].reshape(DB, A_KV_HEADS, HEAD_DIM)
        va_new = ps['vf'][:, :AKV].reshape(DB, A_KV_HEADS, HEAD_DIM)
        kb_new = ps['k128f'][:, AKV:].reshape(DB, 2 * B_HEADS, HEAD_DIM)
        vb_new = ps['vf'][:, AKV:].reshape(DB, B_HEADS, B_V_DIM)
        ki_new = ps['misc'][:, :IDX_DIM]
        oA = dsa_sample(l, page_table,
                        ps['qi'].reshape(DB, IDX_HEADS, IDX_DIM),
                        ps['misc'][:, MISC_WI:MISC_WI + IDX_HEADS].reshape(DB, IDX_HEADS, 1),
                        ki_new.reshape(DB, 1, IDX_DIM),
                        ps['q128'][:, :A_HEADS * HEAD_DIM].reshape(DB, A_HEADS, HEAD_DIM),
                        jnp.repeat(ka_new, A_REP, axis=1), jnp.repeat(va_new, A_REP, axis=1),
                        ci_kt, ca_k, ca_v)
        oB = diff_sample(l, page_table, ps['q128'][:, A_HEADS * HEAD_DIM:].reshape(DB, 2 * B_HEADS, HEAD_DIM),
                         kb_new, jnp.repeat(vb_new, 2, axis=1), cb_k, cb_v, *lam_args)
        padt = lambda a: jnp.pad(a.reshape(DB, Ts, -1), ((0, 0), (0, TS_PAD - Ts), (0, 0)))
        oC, Ss = gla(padt(ps['c']), padt(ps['misc']), w2pad, gla_b[l], gla_norm[l],
                     jnp.swapaxes(state_gla[l], -1, -2), t_valid=Ts)
        xs, xs16 = _tail(xs, xs16, oA.reshape(DB, -1).astype(BF16), oB.reshape(DB, -1).astype(BF16),
                         oC[:, :Ts].reshape(DB * Ts, -1), lw, alpha)
        outs['a_k_s'].append(ka_new.reshape(DB, Ts, A_KV_HEADS, HEAD_DIM))
        outs['a_v_s'].append(va_new.reshape(DB, Ts, A_KV_HEADS, HEAD_DIM))
        outs['i_k_s'].append(ki_new.reshape(DB, Ts, IDX_DIM))
        outs['b_k_s'].append(kb_new.reshape(DB, Ts, B_HEADS, 2, HEAD_DIM))
        outs['b_v_s'].append(vb_new.reshape(DB, Ts, B_HEADS, B_V_DIM))
        outs['g_s'].append(jnp.swapaxes(Ss, -1, -2))

    st = {k: jnp.stack(v) for k, v in outs.items()}
    st['a_k_p'] = cache_p['a_k'].reshape(DEPTH, B, T, A_KV_HEADS, HEAD_DIM)
    st['a_v_p'] = cache_p['a_v'].reshape(DEPTH, B, T, A_KV_HEADS, HEAD_DIM)
    st['b_k_p'] = cache_p['b_k'].reshape(DEPTH, B, T, B_HEADS, 2, HEAD_DIM)
    st['b_v_p'] = jnp.transpose(cache_p['b_v'].reshape(DEPTH, B, T, B_V_DIM // LANES, B_HEADS, LANES),
                                (0, 1, 2, 4, 3, 5)).reshape(DEPTH, B, T, B_HEADS, B_V_DIM)
    return (xp.reshape(B, T, D), xs.reshape(DB, Ts, D),
            st['a_k_p'], st['a_k_s'], st['a_v_p'], st['a_v_s'], st['i_k_p'], st['i_k_s'],
            st['b_k_p'], st['b_k_s'], st['b_v_p'], st['b_v_s'], st['g_p'], st['g_s'])
```

```python
import functools
import math

import jax
import jax.numpy as jnp
from jax import lax
from jax.experimental import pallas as pl
from jax.experimental.pallas import tpu as pltpu

F32 = jnp.float32
BF16 = jnp.bfloat16
I32 = jnp.int32
I16 = jnp.int16

LANES = 128
HEAD_DIM = 128
A_HEADS = 8
A_KV_HEADS = 2
A_REP = A_HEADS // A_KV_HEADS
IDX_HEADS = 16
IDX_DIM = 64
TOPK_MAX = 256
B_HEADS = 4
B_V_DIM = 256
C_HEADS = 4
C_V_DIM = 256
C_K_DIM = 128
GLA_RANK = 16
GLA_TAU = 16.0
GLA_CHUNK = 32
PAGE_SIZE = 128
Q_BLOCK = 128
ROPE_THETA = 10000.0
LN_EPS = 1e-5
RMS_EPS = 1e-6
BRANCH_WIDTH = 1024
D_MODEL = 2048

NEG = -1e30
INT_MIN = -(2 ** 31)
VMEM_LIMIT = 56 * 1024 * 1024

NT_DIMS = (((1,), (1,)), ((), ()))
TN_DIMS = (((0,), (0,)), ((), ()))

MISC_WI = IDX_DIM
MISC_GC = IDX_DIM + IDX_HEADS

SRC_SEGMENTS = (
    ('qa', 1024), ('ka', 256), ('va', 256), ('qi', 1024), ('ki', 64), ('wi', 16),
    ('qb', 1024), ('kb', 1024), ('vb', 1024), ('qc', 512), ('kc', 512), ('vc', 1024),
    ('gc', 16), ('rc', 1024), ('gate', 6144),
)


def _params(sem):
    return pltpu.CompilerParams(dimension_semantics=sem, vmem_limit_bytes=VMEM_LIMIT)


def _sortable(x):
    bits = lax.bitcast_convert_type(x, I32)
    return jnp.where(bits < 0, bits ^ jnp.int32(0x7FFFFFFF), bits)


def _swap_halves(blk, dh):
    if dh == LANES:
        return pltpu.roll(blk, LANES // 2, axis=1)
    lane = lax.broadcasted_iota(I32, blk.shape, 1)
    half = dh // 2
    return jnp.where(lane % dh < half, pltpu.roll(blk, LANES - half, axis=1), pltpu.roll(blk, half, axis=1))


def _proj_body(x_ref, w_ref, *rest, rope, scale, emit_kz):
    if rope:
        cos_ref, sin_ref = rest[:2]
        outs = rest[2:]
    else:
        outs = rest
    r = jnp.dot(x_ref[...], w_ref[...], preferred_element_type=F32)
    tn = r.shape[1]
    if emit_kz:
        o_ref, kz_ref = outs
        y = r * cos_ref[...] + _swap_halves(r, rope) * sin_ref[...]
        o_ref[...] = y
        lane = lax.broadcasted_iota(I32, y.shape, 1)
        kz0 = jnp.where(lane < IDX_DIM, y, 0.0)
        kz_ref[:, :LANES] = kz0.astype(BF16)
        kz_ref[:, LANES:] = pltpu.roll(kz0, IDX_DIM, axis=1).astype(BF16)
        return
    if rope:
        cos = cos_ref[...]
        sin = sin_ref[...]
        for g in range(tn // LANES):
            sl = slice(g * LANES, (g + 1) * LANES)
            blk = r[:, sl]
            y = blk * cos + _swap_halves(blk, rope) * sin
            if scale != 1.0:
                y = y * scale
            for o in outs:
                o[:, sl] = y.astype(o.dtype)
        return
    if scale != 1.0:
        r = r * scale
    for o in outs:
        o[...] = r.astype(o.dtype)


def proj(x, w, *, out_dtypes, rope=None, tables=None, scale=1.0, emit_kz=False, tm=1024, tn=1024, name='proj'):
    M, K = x.shape
    _, N = w.shape
    tm, tn = min(tm, M), min(tn, N)
    assert M % tm == 0 and N % tn == 0
    in_specs = [pl.BlockSpec((tm, K), lambda n, m: (m, 0)), pl.BlockSpec((K, tn), lambda n, m: (0, n))]
    args = [x, w]
    if rope:
        in_specs += [pl.BlockSpec((tm, LANES), lambda n, m: (m, 0))] * 2
        args += list(tables)
    out_specs = [pl.BlockSpec((tm, tn), lambda n, m: (m, n)) for _ in out_dtypes]
    out_shape = [jax.ShapeDtypeStruct((M, N), dt) for dt in out_dtypes]
    if emit_kz:
        out_specs.append(pl.BlockSpec((tm, 2 * LANES), lambda n, m: (m, 0)))
        out_shape.append(jax.ShapeDtypeStruct((M, 2 * LANES), BF16))
    return pl.pallas_call(
        functools.partial(_proj_body, rope=rope, scale=scale, emit_kz=emit_kz),
        grid=(N // tn, M // tm),
        in_specs=in_specs, out_specs=out_specs, out_shape=out_shape,
        compiler_params=_params(('parallel', 'parallel')),
        name=name,
    )(*args)


def _proj_cache_body(x_ref, w_ref, *rest, rope, b_rows):
    if rope:
        cos_ref, sin_ref = rest[:2]
        rest = rest[2:]
    ob_ref, oa_ref, og_ref = rest[-3:]
    tm = x_ref.shape[0]
    r = jnp.dot(x_ref[...], w_ref[...], preferred_element_type=F32)
    for g in range(r.shape[1] // LANES):
        sl = slice(g * LANES, (g + 1) * LANES)
        y = r[:, sl]
        if rope:
            y = y * cos_ref[...] + _swap_halves(y, rope) * sin_ref[...]
        ob_ref[:, sl] = y.astype(BF16)
        if g < A_KV_HEADS:
            oa_ref[pl.ds(g, tm, stride=A_KV_HEADS), :] = y
        else:
            og_ref[pl.ds(b_rows[g - A_KV_HEADS], tm, stride=len(b_rows)), :] = y


def proj_cache(x, w, bufs, layer, depth, *, b_rows, rope=None, tables=None, tm=1024, name='proj_cache'):
    M, K = x.shape
    _, N = w.shape
    tm = min(tm, M)
    nb = M // tm
    ng = len(b_rows)
    in_specs = [pl.BlockSpec((tm, K), lambda m: (m, 0)), pl.BlockSpec((K, N), lambda m: (0, 0))]
    args = [x, w]
    if rope:
        in_specs += [pl.BlockSpec((tm, LANES), lambda m: (m, 0))] * 2
        args += list(tables)
    aliases = {len(args): 1, len(args) + 1: 2}
    in_specs += [pl.BlockSpec(memory_space=pl.ANY)] * 2
    args += list(bufs)
    ob, buf_a, buf_g = pl.pallas_call(
        functools.partial(_proj_cache_body, rope=rope, b_rows=tuple(b_rows)),
        grid=(nb,),
        in_specs=in_specs,
        out_specs=[pl.BlockSpec((tm, N), lambda m: (m, 0)),
                   pl.BlockSpec((tm * A_KV_HEADS, LANES), lambda m: (layer * nb + m, 0)),
                   pl.BlockSpec((tm * ng, LANES), lambda m: (layer * nb + m, 0))],
        out_shape=[jax.ShapeDtypeStruct((M, N), BF16),
                   jax.ShapeDtypeStruct((depth * M * A_KV_HEADS, LANES), F32),
                   jax.ShapeDtypeStruct((depth * M * ng, LANES), F32)],
        input_output_aliases=aliases,
        compiler_params=_params(('arbitrary',)),
        name=name,
    )(*args)
    return ob, (buf_a, buf_g)


def _mm_act_body(x_ref, w_ref, o_ref):
    r = jnp.dot(x_ref[...], w_ref[...], preferred_element_type=F32)
    o_ref[...] = jnp.square(jnp.maximum(r, 0.0)).astype(o_ref.dtype)


def mlp_up(x, w, *, tm, tn):
    M, K = x.shape
    _, N = w.shape
    tm, tn = min(tm, M), min(tn, N)
    return pl.pallas_call(
        _mm_act_body,
        grid=(N // tn, M // tm),
        in_specs=[pl.BlockSpec((tm, K), lambda n, m: (m, 0)), pl.BlockSpec((K, tn), lambda n, m: (0, n))],
        out_specs=pl.BlockSpec((tm, tn), lambda n, m: (m, n)),
        out_shape=jax.ShapeDtypeStruct((M, N), BF16),
        compiler_params=_params(('parallel', 'parallel')),
        name='mlp_up',
    )(x, w)


def _mm_ln_body(x_ref, w_ref, r_ref, g_ref, b_ref, o_ref, ob_ref, acc_ref, *, nk, alpha):
    k = pl.program_id(1)

    @pl.when(k == 0)
    def _():
        acc_ref[...] = jnp.zeros_like(acc_ref)

    acc_ref[...] += jnp.dot(x_ref[...], w_ref[...], preferred_element_type=F32)

    @pl.when(k == nk - 1)
    def _():
        y = alpha * r_ref[...] + acc_ref[...]
        mu = jnp.mean(y, axis=-1, keepdims=True)
        yc = y - mu
        var = jnp.mean(yc * yc, axis=-1, keepdims=True)
        out = yc * lax.rsqrt(var + LN_EPS) * g_ref[...] + b_ref[...]
        o_ref[...] = out
        ob_ref[...] = out.astype(BF16)


def matmul_ln(x, w, resid, g, b, *, alpha, tm, tk, name='mm_ln'):
    M, K = x.shape
    _, N = w.shape
    tm, tk = min(tm, M), min(tk, K)
    nk = K // tk
    return pl.pallas_call(
        functools.partial(_mm_ln_body, nk=nk, alpha=alpha),
        grid=(M // tm, nk),
        in_specs=[pl.BlockSpec((tm, tk), lambda m, k: (m, k)),
                  pl.BlockSpec((tk, N), lambda m, k: (k, 0)),
                  pl.BlockSpec((tm, N), lambda m, k: (m, 0)),
                  pl.BlockSpec((1, N), lambda m, k: (0, 0)),
                  pl.BlockSpec((1, N), lambda m, k: (0, 0))],
        out_specs=[pl.BlockSpec((tm, N), lambda m, k: (m, 0)),
                   pl.BlockSpec((tm, N), lambda m, k: (m, 0))],
        out_shape=[jax.ShapeDtypeStruct((M, N), F32), jax.ShapeDtypeStruct((M, N), BF16)],
        scratch_shapes=[pltpu.VMEM((tm, N), F32)],
        compiler_params=_params(('parallel', 'arbitrary')),
        name=name,
    )(x, w, resid, g.reshape(1, N), b.reshape(1, N))


def _branch_body(x_ref, a_ref, b_ref, c_ref, wb_ref, wga_ref, wgb_ref, wgc_ref, o_ref):
    x = x_ref[...]
    acc = None
    for n, (br, wg) in enumerate(((a_ref, wga_ref), (b_ref, wgb_ref), (c_ref, wgc_ref))):
        gate = jax.nn.sigmoid(jnp.dot(x, wg[...], preferred_element_type=F32))
        t = gate * jnp.dot(br[...], wb_ref[n], preferred_element_type=F32)
        acc = t if acc is None else acc + t
    o_ref[...] = acc.astype(o_ref.dtype)


def branch_merge(x16, brA, brB, brC, w_branch, w_gate, *, tm, tn):
    M = x16.shape[0]
    tm = min(tm, M)
    nb = D_MODEL // tn
    gspecs = [pl.BlockSpec((D_MODEL, tn), functools.partial(lambda m, n, base: (0, base + n), base=i * nb))
              for i in range(3)]
    return pl.pallas_call(
        _branch_body,
        grid=(M // tm, nb),
        in_specs=[pl.BlockSpec((tm, D_MODEL), lambda m, n: (m, 0))]
        + [pl.BlockSpec((tm, BRANCH_WIDTH), lambda m, n: (m, 0))] * 3
        + [pl.BlockSpec((3, BRANCH_WIDTH, tn), lambda m, n: (0, 0, n))] + gspecs,
        out_specs=pl.BlockSpec((tm, tn), lambda m, n: (m, n)),
        out_shape=jax.ShapeDtypeStruct((M, D_MODEL), BF16),
        compiler_params=_params(('parallel', 'parallel')),
        name='branch_merge',
    )(x16, brA, brB, brC, w_branch, w_gate, w_gate, w_gate)


def _dsa_prompt_body(qi_ref, misc_ref, kz_ref, qa_ref, ka_ref, va_ref, o_ref, keys_ref, bias_ref, hi_ref, lo_ref,
                     *, top_k, tk, T, qrows):
    QB = qrows
    qb = pl.program_id(1)
    q0 = qb * QB
    nkt = (q0 + QB - 1) // tk + 1
    qpos = q0 + lax.broadcasted_iota(I32, (QB, tk), 0)
    lane = lax.broadcasted_iota(I32, (QB, tk), 1)
    qpos_c = q0 + lax.broadcasted_iota(I32, (QB, LANES), 0)
    lane_c = lax.broadcasted_iota(I32, (QB, LANES), 1)
    wi = misc_ref[:, MISC_WI:MISC_WI + IDX_HEADS]

    def score_tile(j, carry):
        off = pl.multiple_of(j * tk, tk)
        kz = (kz_ref[pl.ds(off, tk), :LANES], kz_ref[pl.ds(off, tk), LANES:])
        acc = jnp.zeros((QB, tk), F32)
        for h in range(IDX_HEADS):
            qp = qi_ref[:, (h // 2) * LANES:(h // 2 + 1) * LANES]
            s = lax.dot_general(qp, kz[h % 2], NT_DIMS, preferred_element_type=F32)
            acc = acc + wi[:, h:h + 1] * jnp.maximum(s, 0.0)
        key = jnp.where(off + lane <= qpos, _sortable(acc), INT_MIN)
        keys_ref[:, pl.ds(off, tk)] = key
        hi_ref[:, pl.ds(off, tk)] = (key >> 16).astype(I16)
        return carry

    lax.fori_loop(0, nkt, score_tile, 0)

    def count16(ref, pred):
        def body(j, acc):
            off = pl.multiple_of(j * tk, tk)
            for c in range(tk // LANES):
                kc = ref[:, pl.ds(off + c * LANES, LANES)]
                acc = acc + jnp.where(pred(kc), jnp.int16(1), jnp.int16(0))
            return acc
        acc = lax.fori_loop(0, nkt, body, jnp.zeros((QB, LANES), I16))
        return jnp.sum(acc.astype(F32), axis=1, keepdims=True)

    def bcast16(v):
        return jnp.broadcast_to(v, (QB, LANES)).astype(I16)

    def kth_largest16(ref, kneed):
        zero_b = jnp.zeros((QB, LANES), I16)
        lo = jnp.where(count16(ref, lambda kc: kc >= zero_b) >= kneed, jnp.int32(0), jnp.int32(-(2 ** 15)))

        def bit_body(i, lo):
            cand = lo + (jnp.int32(1) << (14 - i))
            cand_b = bcast16(cand)
            return jnp.where(count16(ref, lambda kc: kc >= cand_b) >= kneed, cand, lo)

        return lax.fori_loop(0, 15, bit_body, lo)

    def count(pred):
        def body(j, acc):
            off = pl.multiple_of(j * tk, tk)
            for c in range(tk // LANES):
                kc = keys_ref[:, pl.ds(off + c * LANES, LANES)]
                acc = acc + jnp.where(pred(kc, off + c * LANES + lane_c), 1.0, 0.0)
            return acc
        acc = lax.fori_loop(0, nkt, body, jnp.zeros((QB, LANES), F32))
        return jnp.sum(acc, axis=1, keepdims=True)

    def bcast(v):
        return jnp.broadcast_to(v, (QB, LANES))

    kf = float(top_k)
    t_hi = kth_largest16(hi_ref, kf)
    t_hi16 = bcast16(t_hi)
    need_lo = kf - count16(hi_ref, lambda kc: kc > t_hi16)
    t_hi_b = bcast(t_hi)

    def low_tile(j, carry):
        off = pl.multiple_of(j * tk, tk)
        for c in range(tk // LANES):
            kc = keys_ref[:, pl.ds(off + c * LANES, LANES)]
            lo16 = (kc & jnp.int32(0xFFFF)) - jnp.int32(2 ** 15)
            lo_ref[:, pl.ds(off + c * LANES, LANES)] = jnp.where((kc >> 16) == t_hi_b, lo16, jnp.int32(-(2 ** 15))).astype(I16)
        return carry

    lax.fori_loop(0, nkt, low_tile, 0)
    t_lo = kth_largest16(lo_ref, need_lo)
    thr = t_hi * jnp.int32(2 ** 16) + (t_lo + jnp.int32(2 ** 15))
    thr_b = bcast(thr)

    need = kf - count(lambda kc, kp: kc > thr_b)
    ceq = count(lambda kc, kp: kc == thr_b)
    nbits = max(T.bit_length(), 1)

    def tie_fn():
        def jb(i, j):
            cand = j + (jnp.int32(1) << (nbits - 1 - i))
            cand_b = bcast(cand)
            g = count(lambda kc, kp: (kc == thr_b) & (kp < cand_b))
            return jnp.where(g <= need, cand, j)
        return lax.fori_loop(0, nbits, jb, jnp.zeros((QB, 1), I32))

    jstar = lax.cond(jnp.max(ceq - need) > 0.0, tie_fn, lambda: jnp.full((QB, 1), 2 ** 30, I32))
    jstar_b = bcast(jstar)

    def bias_tile(j, carry):
        off = pl.multiple_of(j * tk, tk)
        for c in range(tk // LANES):
            kc = keys_ref[:, pl.ds(off + c * LANES, LANES)]
            kp = off + c * LANES + lane_c
            sel = ((kc > thr_b) | ((kc == thr_b) & (kp < jstar_b))) & (kp <= qpos_c)
            bias_ref[:, pl.ds(off + c * LANES, LANES)] = jnp.where(sel, 0.0, NEG)
        return carry

    lax.fori_loop(0, nkt, bias_tile, 0)

    AB = Q_BLOCK
    for sb in range(QB // AB):
        rs = slice(sb * AB, (sb + 1) * AB)
        nkt_s = (q0 + (sb + 1) * AB - 1) // tk + 1
        qgs = [jnp.concatenate([qa_ref[rs, (A_REP * g + r) * HEAD_DIM:(A_REP * g + r + 1) * HEAD_DIM]
                                for r in range(A_REP)], axis=0) for g in range(A_KV_HEADS)]

        def att_tile(j, carry, rs=rs, qgs=qgs):
            off = pl.multiple_of(j * tk, tk)
            b = bias_ref[rs, pl.ds(off, tk)]
            new = []
            for g in range(A_KV_HEADS):
                m, l, acc = carry[g]
                gs = slice(g * HEAD_DIM, (g + 1) * HEAD_DIM)
                kt = ka_ref[pl.ds(off, tk), gs]
                vt = va_ref[pl.ds(off, tk), gs]
                s = lax.dot_general(qgs[g], kt, NT_DIMS, preferred_element_type=F32)
                s = (s.reshape(A_REP, AB, tk) + b[None]).reshape(A_REP * AB, tk)
                m_new = jnp.maximum(m, jnp.max(s, axis=1, keepdims=True))
                corr = jnp.exp(m - m_new)
                p = jnp.exp(s - m_new)
                l = l * corr + jnp.sum(p, axis=1, keepdims=True)
                acc = acc * corr + jnp.dot(p.astype(BF16), vt, preferred_element_type=F32)
                new.append((m_new, l, acc))
            return tuple(new)

        one = (jnp.full((A_REP * AB, 1), NEG, F32), jnp.zeros((A_REP * AB, 1), F32),
               jnp.zeros((A_REP * AB, HEAD_DIM), F32))
        carry = lax.fori_loop(0, nkt_s // 2, lambda j2, c, f=att_tile: f(2 * j2 + 1, f(2 * j2, c)),
                              (one,) * A_KV_HEADS)
        carry = lax.fori_loop(2 * (nkt_s // 2), nkt_s, att_tile, carry)
        for g in range(A_KV_HEADS):
            _, l, acc = carry[g]
            out = acc / l
            for r in range(A_REP):
                h = A_REP * g + r
                o_ref[rs, h * HEAD_DIM:(h + 1) * HEAD_DIM] = out[r * AB:(r + 1) * AB].astype(o_ref.dtype)


def dsa_prompt(qi, misc, kz, q128, k128, v16):
    B, T, _ = qi.shape
    top_k = min(TOPK_MAX, T // 4)
    tk = min(512, T)
    qrows = min(Q_BLOCK, T)
    nb = T // qrows
    kvw = A_KV_HEADS * HEAD_DIM
    return pl.pallas_call(
        functools.partial(_dsa_prompt_body, top_k=top_k, tk=tk, T=T, qrows=qrows),
        grid=(B, nb),
        in_specs=[pl.BlockSpec((None, qrows, IDX_HEADS * IDX_DIM), lambda b, q: (b, q, 0)),
                  pl.BlockSpec((None, qrows, LANES), lambda b, q: (b, q, 0)),
                  pl.BlockSpec((None, T, 2 * LANES), lambda b, q: (b, 0, 0)),
                  pl.BlockSpec((None, qrows, A_HEADS * HEAD_DIM), lambda b, q: (b, q, 0)),
                  pl.BlockSpec((None, T, kvw), lambda b, q: (b, 0, 0)),
                  pl.BlockSpec((None, T, kvw), lambda b, q: (b, 0, 0))],
        out_specs=pl.BlockSpec((None, qrows, A_HEADS * HEAD_DIM), lambda b, q: (b, q, 0)),
        out_shape=jax.ShapeDtypeStruct((B, T, A_HEADS * HEAD_DIM), BF16),
        scratch_shapes=[pltpu.VMEM((qrows, T), I32), pltpu.VMEM((qrows, T), F32),
                        pltpu.VMEM((qrows, T), I16), pltpu.VMEM((qrows, T), I16)],
        compiler_params=_params(('parallel', 'arbitrary')),
        name='dsa_prompt',
    )(qi, misc, kz, q128, k128, v16)


def _lam(lq1, lk1, lq2, lk2, lam_init):
    return (jnp.exp(jnp.sum(lq1[...] * lk1[...], axis=-1, keepdims=True))
            - jnp.exp(jnp.sum(lq2[...] * lk2[...], axis=-1, keepdims=True)) + lam_init)


def _diff_finish(o0, o1, lam, subln, lam_init):
    a = o0 - lam * o1
    a = a * lax.rsqrt(jnp.mean(a * a, axis=-1, keepdims=True) + RMS_EPS) * subln
    return a * (1.0 - lam_init)


def _diff_prompt_body(lq1, lk1, lq2, lk2, sub_ref, q_ref, k_ref, v_ref, o_ref, *, tq, tk, lam_init):
    qi = pl.program_id(2)
    q0 = qi * tq
    n_full = q0 // tk
    n_all = (q0 + tq - 1) // tk + 1
    qpos = q0 + lax.broadcasted_iota(I32, (tq, tk), 0)
    lane = lax.broadcasted_iota(I32, (tq, tk), 1)
    qs = [q_ref[:, mp * HEAD_DIM:(mp + 1) * HEAD_DIM] for mp in range(2)]

    def tile(j, carry, masked):
        off = pl.multiple_of(j * tk, tk)
        vt = v_ref[pl.ds(off, tk), :]
        new = []
        for mp in range(2):
            m, l, acc = carry[mp]
            kt = k_ref[pl.ds(off, tk), mp * HEAD_DIM:(mp + 1) * HEAD_DIM]
            s = lax.dot_general(qs[mp], kt, NT_DIMS, preferred_element_type=F32)
            if masked:
                s = jnp.where(off + lane <= qpos, s, NEG)
            m_new = jnp.maximum(m, jnp.max(s, axis=1, keepdims=True))
            corr = jnp.exp(m - m_new)
            p = jnp.exp(s - m_new)
            l = l * corr + jnp.sum(p, axis=1, keepdims=True)
            acc = acc * corr + jnp.dot(p.astype(BF16), vt, preferred_element_type=F32)
            new.append((m_new, l, acc))
        return tuple(new)

    one = (jnp.full((tq, 1), NEG, F32), jnp.zeros((tq, 1), F32), jnp.zeros((tq, B_V_DIM), F32))
    def tile2(j2, carry):
        return tile(2 * j2 + 1, tile(2 * j2, carry, masked=False), masked=False)

    carry = lax.fori_loop(0, n_full // 2, tile2, (one, one))
    carry = lax.fori_loop(2 * (n_full // 2), n_full, functools.partial(tile, masked=False), carry)
    carry = lax.fori_loop(n_full, n_all, functools.partial(tile, masked=True), carry)
    outs = [acc / l for (_, l, acc) in carry]
    lam = _lam(lq1, lk1, lq2, lk2, lam_init)
    o_ref[...] = _diff_finish(outs[0], outs[1], lam, sub_ref[...], lam_init).astype(o_ref.dtype)


def diff_prompt(q128, k128, v16, lam_q1, lam_k1, lam_q2, lam_k2, subln, lam_init):
    B, T, _ = q128.shape
    H = B_HEADS
    tq = min(512, T)
    tk = min(512, T)
    pw = 2 * HEAD_DIM
    qb0 = (A_HEADS * HEAD_DIM) // pw
    kb0 = (A_KV_HEADS * HEAD_DIM) // pw
    vec = pl.BlockSpec((1, HEAD_DIM), lambda b, h, q: (0, 0))
    return pl.pallas_call(
        functools.partial(_diff_prompt_body, tq=tq, tk=tk, lam_init=lam_init),
        grid=(B, H, T // tq),
        in_specs=[vec, vec, vec, vec,
                  pl.BlockSpec((1, B_V_DIM), lambda b, h, q: (0, 0)),
                  pl.BlockSpec((None, tq, pw), lambda b, h, q: (b, q, qb0 + h)),
                  pl.BlockSpec((None, T, pw), lambda b, h, q: (b, 0, kb0 + h)),
                  pl.BlockSpec((None, T, B_V_DIM), lambda b, h, q: (b, 0, kb0 + h))],
        out_specs=pl.BlockSpec((None, tq, B_V_DIM), lambda b, h, q: (b, q, h)),
        out_shape=jax.ShapeDtypeStruct((B, T, H * B_V_DIM), BF16),
        compiler_params=_params(('parallel', 'parallel', 'arbitrary')),
        name='diff_prompt',
    )(lam_q1.reshape(1, -1), lam_k1.reshape(1, -1), lam_q2.reshape(1, -1), lam_k2.reshape(1, -1),
      subln.reshape(1, -1), q128, k128, v16)


def _gla_body(q_ref, k_ref, v_ref, rc_ref, misc_ref, w2_ref, gb_ref, gn_ref, s0_ref, o_ref, sfin_ref, st_ref,
              *, tb, chunk, t_valid, nt):
    t = pl.program_id(1)

    @pl.when(t == 0)
    def _():
        st_ref[...] = s0_ref[...]

    x = jnp.dot(misc_ref[...].astype(BF16), w2_ref[...], preferred_element_type=F32) + gb_ref[...]
    la = (jnp.minimum(x, 0.0) - jnp.log(1.0 + jnp.exp(-jnp.abs(x)))) * (1.0 / GLA_TAU)
    W = C_HEADS * C_K_DIM
    row = lax.broadcasted_iota(I32, (tb, W), 0)
    if t_valid is not None:
        la = jnp.where(t * tb + row < t_valid, la, 0.0)
    rowc = row % chunk
    b = la
    sh = 1
    while sh < chunk:
        b = b + jnp.where(rowc >= sh, pltpu.roll(b, sh, axis=0), 0.0)
        sh *= 2
    k = k_ref[...]
    qe = (q_ref[...] * (C_K_DIM ** -0.5) * jnp.exp(b)).astype(BF16)
    ke = (k * jnp.exp(-b)).astype(BF16)
    v16 = v_ref[...].astype(BF16)
    rc = rc_ref[...]
    gn = gn_ref[...]
    tril = lax.broadcasted_iota(I32, (chunk, chunk), 0) >= lax.broadcasted_iota(I32, (chunk, chunk), 1)
    for c in range(tb // chunk):
        r0 = c * chunk
        bl = b[r0 + chunk - 1:r0 + chunk, :]
        kd = (k[r0:r0 + chunk] * jnp.exp(bl - b[r0:r0 + chunk])).astype(BF16)
        dec = jnp.exp(bl)
        for h in range(C_HEADS):
            ck = slice(h * C_K_DIM, (h + 1) * C_K_DIM)
            cv = slice(h * C_V_DIM, (h + 1) * C_V_DIM)
            qe_c = qe[r0:r0 + chunk, ck]
            v_c = v16[r0:r0 + chunk, cv]
            att = lax.dot_general(qe_c, ke[r0:r0 + chunk, ck], NT_DIMS, preferred_element_type=F32)
            att = jnp.where(tril, att, 0.0).astype(BF16)
            st = st_ref[h]
            o_c = (lax.dot_general(qe_c, st.astype(BF16), NT_DIMS, preferred_element_type=F32)
                   + jnp.dot(att, v_c, preferred_element_type=F32))
            st_ref[h] = st * dec[:, ck] + lax.dot_general(v_c, kd[:, ck], TN_DIMS, preferred_element_type=F32)
            o_n = o_c * lax.rsqrt(jnp.mean(o_c * o_c, axis=-1, keepdims=True) + RMS_EPS) * gn
            r_c = rc[r0:r0 + chunk, cv]
            o_ref[r0:r0 + chunk, cv] = (o_n * (r_c * jax.nn.sigmoid(r_c))).astype(o_ref.dtype)

    @pl.when(t == nt - 1)
    def _():
        sfin_ref[...] = st_ref[...]


def gla(cproj, misc, w2pad, gb, gn, s0t, *, t_valid=None):
    B, T, _ = cproj.shape
    chunk = GLA_CHUNK
    tb = min(256, T)
    nt = T // tb
    W = C_HEADS * C_K_DIM
    V = C_HEADS * C_V_DIM
    st_spec = pl.BlockSpec((None, C_HEADS, C_V_DIM, C_K_DIM), lambda b, t: (b, 0, 0, 0))
    return pl.pallas_call(
        functools.partial(_gla_body, tb=tb, chunk=chunk, t_valid=t_valid, nt=nt),
        grid=(B, nt),
        in_specs=[pl.BlockSpec((None, tb, W), lambda b, t: (b, t, 0)),
                  pl.BlockSpec((None, tb, W), lambda b, t: (b, t, 1)),
                  pl.BlockSpec((None, tb, V), lambda b, t: (b, t, 1)),
                  pl.BlockSpec((None, tb, V), lambda b, t: (b, t, 2)),
                  pl.BlockSpec((None, tb, LANES), lambda b, t: (b, t, 0)),
                  pl.BlockSpec((LANES, W), lambda b, t: (0, 0)),
                  pl.BlockSpec((1, W), lambda b, t: (0, 0)),
                  pl.BlockSpec((1, C_V_DIM), lambda b, t: (0, 0)),
                  st_spec],
        out_specs=[pl.BlockSpec((None, tb, V), lambda b, t: (b, t, 0)), st_spec],
        out_shape=[jax.ShapeDtypeStruct((B, T, V), BF16),
                   jax.ShapeDtypeStruct((B, C_HEADS, C_V_DIM, C_K_DIM), F32)],
        scratch_shapes=[pltpu.VMEM((C_HEADS, C_V_DIM, C_K_DIM), F32)],
        compiler_params=_params(('parallel', 'arbitrary')),
        name='gla',
    )(cproj, cproj, cproj, cproj, misc, w2pad, gb.reshape(1, W), gn.reshape(1, C_V_DIM), s0t)


def _page_specs(block, layer, pps):
    def mk(i):
        return pl.BlockSpec((None, None) + block,
                            lambda b, p, pt: (layer, pt[b, p * pps + i]) + (0,) * len(block))
    return [mk(i) for i in range(pps)]


def _dsa_sample_score_body(pt_ref, qi_ref, wi_ref, kin_ref, *rest, pps, n_pages):
    pages, o_ref = rest[:pps], rest[pps]
    p = pl.program_id(1)
    qi = qi_ref[...]
    wi = wi_ref[...]

    @pl.when(p == 0)
    def _():
        o_ref[...] = jnp.full(o_ref.shape, -jnp.inf, F32)
        s = jnp.sum(qi.astype(F32) * kin_ref[...], axis=1, keepdims=True)
        snew = jnp.sum(wi * jnp.maximum(s, 0.0), axis=0, keepdims=True)
        lane = lax.broadcasted_iota(I32, (1, PAGE_SIZE), 1)
        o_ref[n_pages:n_pages + 1, :] = jnp.where(lane == 0, snew, -jnp.inf)

    kpt = jnp.concatenate([pg[...].astype(BF16) for pg in pages], axis=1)
    s = jnp.dot(qi, kpt, preferred_element_type=F32)
    sc = jnp.sum(wi * jnp.maximum(s, 0.0), axis=0, keepdims=True)
    for i in range(pps):
        o_ref[pl.ds(p * pps + i, 1), :] = sc[:, i * PAGE_SIZE:(i + 1) * PAGE_SIZE]


def _topk_bias(sc, top_k, n_valid):
    R = sc.shape[0]
    key = _sortable(sc)
    pos = lax.broadcasted_iota(I32, sc.shape, 0) * PAGE_SIZE + lax.broadcasted_iota(I32, sc.shape, 1)
    key = jnp.where(pos < n_valid, key, INT_MIN)

    def count(hit):
        c = jnp.sum(jnp.where(hit, 1.0, 0.0), axis=1, keepdims=True)
        return jnp.sum(c, axis=0, keepdims=True)

    kf = float(top_k)
    lo = jnp.where(count(key >= 0) >= kf, jnp.int32(0), jnp.int32(INT_MIN))

    def bit_body(i, lo):
        cand = lo + (jnp.int32(1) << (30 - i))
        return jnp.where(count(key >= cand) >= kf, cand, lo)

    thr = lax.fori_loop(0, 31, bit_body, lo)
    need = kf - count(key > thr)
    nbits = (R * PAGE_SIZE).bit_length()

    def jb(i, j):
        cand = j + (jnp.int32(1) << (nbits - 1 - i))
        return jnp.where(count((key == thr) & (pos < cand)) <= need, cand, j)

    jstar = lax.fori_loop(0, nbits, jb, jnp.zeros((1, 1), I32))
    sel = ((key > thr) | ((key == thr) & (pos < jstar))) & (pos < n_valid)
    return jnp.where(sel, 0.0, NEG)


def _dsa_sample_attn_body(pt_ref, sc_ref, q_ref, kn_ref, vn_ref, *rest, pps, n_pages, top_k):
    kpages, vpages = rest[:pps], rest[pps:2 * pps]
    o_ref, bias_ref, m_ref, l_ref, acc_ref = rest[2 * pps:]
    p = pl.program_id(1)
    q = q_ref[...]
    first = lax.broadcasted_iota(I32, (A_HEADS, PAGE_SIZE), 0) < A_REP

    @pl.when(p == 0)
    def _():
        bias_ref[...] = _topk_bias(sc_ref[...], top_k, n_pages * PAGE_SIZE + 1)
        m_ref[...] = jnp.full(m_ref.shape, NEG, F32)
        l_ref[...] = jnp.zeros_like(l_ref)
        acc_ref[...] = jnp.zeros_like(acc_ref)

    def kv(refs, g):
        return jnp.concatenate([r[pl.ds(g, PAGE_SIZE, stride=A_KV_HEADS), :].astype(BF16) for r in refs], axis=0)

    W = pps * PAGE_SIZE
    first_w = lax.broadcasted_iota(I32, (A_HEADS, W), 0) < A_REP
    s0 = lax.dot_general(q, kv(kpages, 0), NT_DIMS, preferred_element_type=F32)
    s1 = lax.dot_general(q, kv(kpages, 1), NT_DIMS, preferred_element_type=F32)
    bias = jnp.concatenate([bias_ref[pl.ds(p * pps + i, 1), :] for i in range(pps)], axis=1)
    s = jnp.where(first_w, s0, s1) + bias
    m = m_ref[...]
    m_new = jnp.maximum(m, jnp.max(s, axis=1, keepdims=True))
    corr = jnp.exp(m - m_new)
    pr = jnp.exp(s - m_new)
    l_ref[...] = l_ref[...] * corr + jnp.sum(pr, axis=1, keepdims=True)
    pb = pr.astype(BF16)
    pv = jnp.where(first, jnp.dot(pb, kv(vpages, 0), preferred_element_type=F32),
                   jnp.dot(pb, kv(vpages, 1), preferred_element_type=F32))
    acc_ref[...] = acc_ref[...] * corr + pv
    m_ref[...] = m_new

    @pl.when(p == pl.num_programs(1) - 1)
    def _():
        s = (jnp.sum(q.astype(F32) * kn_ref[...], axis=1, keepdims=True)
             + bias_ref[n_pages:n_pages + 1, 0:1])
        m = m_ref[...]
        m_new = jnp.maximum(m, s)
        corr = jnp.exp(m - m_new)
        pr = jnp.exp(s - m_new)
        l = l_ref[...] * corr + pr
        o_ref[...] = (acc_ref[...] * corr + pr * vn_ref[...]) / l


def dsa_sample(layer, page_table, qi, wi, ki_new, qa, ka_new, va_new, cache_idx_kt, cache_a_k, cache_a_v):
    DB, n_pages = page_table.shape
    pps_s = math.gcd(32, n_pages)
    pps = math.gcd(16, n_pages)
    R = ((n_pages + 1 + 7) // 8) * 8
    L = n_pages * PAGE_SIZE + 1
    top_k = min(TOPK_MAX, L // 4)
    per_b = lambda *blk: pl.BlockSpec((None,) + blk, lambda b, p, pt: (b,) + (0,) * len(blk))
    scores = pl.pallas_call(
        functools.partial(_dsa_sample_score_body, pps=pps_s, n_pages=n_pages),
        grid_spec=pltpu.PrefetchScalarGridSpec(
            num_scalar_prefetch=1, grid=(DB, n_pages // pps_s),
            in_specs=[per_b(IDX_HEADS, IDX_DIM), per_b(IDX_HEADS, 1), per_b(1, IDX_DIM)]
            + _page_specs((IDX_DIM, PAGE_SIZE), layer, pps_s),
            out_specs=per_b(R, PAGE_SIZE)),
        out_shape=jax.ShapeDtypeStruct((DB, R, PAGE_SIZE), F32),
        compiler_params=_params(('parallel', 'arbitrary')),
        name='dsa_sample_scores',
    )(page_table, qi, wi, ki_new, *([cache_idx_kt] * pps_s))
    rows = PAGE_SIZE * A_KV_HEADS
    return pl.pallas_call(
        functools.partial(_dsa_sample_attn_body, pps=pps, n_pages=n_pages, top_k=top_k),
        grid_spec=pltpu.PrefetchScalarGridSpec(
            num_scalar_prefetch=1, grid=(DB, n_pages // pps),
            in_specs=[per_b(R, PAGE_SIZE), per_b(A_HEADS, HEAD_DIM), per_b(A_HEADS, HEAD_DIM), per_b(A_HEADS, HEAD_DIM)]
            + _page_specs((rows, HEAD_DIM), layer, pps) + _page_specs((rows, HEAD_DIM), layer, pps),
            out_specs=per_b(A_HEADS, HEAD_DIM),
            scratch_shapes=[pltpu.VMEM((R, PAGE_SIZE), F32), pltpu.VMEM((A_HEADS, 1), F32),
                            pltpu.VMEM((A_HEADS, 1), F32), pltpu.VMEM((A_HEADS, HEAD_DIM), F32)]),
        out_shape=jax.ShapeDtypeStruct((DB, A_HEADS, HEAD_DIM), F32),
        compiler_params=_params(('parallel', 'arbitrary')),
        name='dsa_sample_attn',
    )(page_table, scores, qa, ka_new, va_new, *([cache_a_k] * pps), *([cache_a_v] * pps))


def _diff_sample_body(pt_ref, lq1, lk1, lq2, lk2, sub_ref, q_ref, kn_ref, vn_ref, *rest, pps, lam_init):
    kpages, vpages = rest[:pps], rest[pps:2 * pps]
    o_ref, m_ref, l_ref, acc_ref = rest[2 * pps:]
    p = pl.program_id(1)
    NJ = 2 * B_HEADS
    q = q_ref[...]
    rowk = lax.broadcasted_iota(I32, (NJ, PAGE_SIZE), 0)

    @pl.when(p == 0)
    def _():
        m_ref[...] = jnp.full(m_ref.shape, NEG, F32)
        l_ref[...] = jnp.zeros_like(l_ref)
        acc_ref[...] = jnp.zeros_like(acc_ref)

    def rows(refs, j):
        return jnp.concatenate([r[pl.ds(j, PAGE_SIZE, stride=NJ), :].astype(BF16) for r in refs], axis=0)

    W = pps * PAGE_SIZE
    roww = lax.broadcasted_iota(I32, (NJ, W), 0)
    s = jnp.zeros((NJ, W), F32)
    for j in range(NJ):
        sj = lax.dot_general(q, rows(kpages, j), NT_DIMS, preferred_element_type=F32)
        s = jnp.where(roww == j, sj, s)
    m = m_ref[...]
    m_new = jnp.maximum(m, jnp.max(s, axis=1, keepdims=True))
    corr = jnp.exp(m - m_new)
    pr = jnp.exp(s - m_new)
    l_ref[...] = l_ref[...] * corr + jnp.sum(pr, axis=1, keepdims=True)
    pb = pr.astype(BF16)
    halves = []
    for c in range(B_V_DIM // LANES):
        pv = jnp.zeros((NJ, LANES), F32)
        for h in range(B_HEADS):
            ph = jnp.dot(pb, rows(vpages, c * B_HEADS + h), preferred_element_type=F32)
            pv = jnp.where(rowk // 2 == h, ph, pv)
        halves.append(pv)
    acc_ref[...] = acc_ref[...] * corr + jnp.concatenate(halves, axis=1)
    m_ref[...] = m_new

    @pl.when(p == pl.num_programs(1) - 1)
    def _():
        s = jnp.sum(q.astype(F32) * kn_ref[...], axis=1, keepdims=True)
        m = m_ref[...]
        m_new = jnp.maximum(m, s)
        corr = jnp.exp(m - m_new)
        pr = jnp.exp(s - m_new)
        l = l_ref[...] * corr + pr
        o = (acc_ref[...] * corr + pr * vn_ref[...]) / l
        lam = _lam(lq1, lk1, lq2, lk2, lam_init)
        for h in range(B_HEADS):
            o_ref[h:h + 1, :] = _diff_finish(o[2 * h:2 * h + 1], o[2 * h + 1:2 * h + 2], lam, sub_ref[...], lam_init)


def diff_sample(layer, page_table, qb, kb_new, vb_new, cache_b_k, cache_b_v,
                lam_q1, lam_k1, lam_q2, lam_k2, subln, lam_init):
    DB, n_pages = page_table.shape
    pps = math.gcd(8, n_pages)
    NJ = 2 * B_HEADS
    per_b = lambda *blk: pl.BlockSpec((None,) + blk, lambda b, p, pt: (b,) + (0,) * len(blk))
    vec = lambda w: pl.BlockSpec((1, w), lambda b, p, pt: (0, 0))
    return pl.pallas_call(
        functools.partial(_diff_sample_body, pps=pps, lam_init=lam_init),
        grid_spec=pltpu.PrefetchScalarGridSpec(
            num_scalar_prefetch=1, grid=(DB, n_pages // pps),
            in_specs=[vec(HEAD_DIM)] * 4 + [vec(B_V_DIM),
                      per_b(NJ, HEAD_DIM), per_b(NJ, HEAD_DIM), per_b(NJ, B_V_DIM)]
            + _page_specs((PAGE_SIZE * NJ, HEAD_DIM), layer, pps)
            + _page_specs((PAGE_SIZE * NJ, LANES), layer, pps),
            out_specs=per_b(B_HEADS, B_V_DIM),
            scratch_shapes=[pltpu.VMEM((NJ, 1), F32), pltpu.VMEM((NJ, 1), F32), pltpu.VMEM((NJ, B_V_DIM), F32)]),
        out_shape=jax.ShapeDtypeStruct((DB, B_HEADS, B_V_DIM), F32),
        compiler_params=_params(('parallel', 'arbitrary')),
        name='diff_sample',
    )(page_table, lam_q1.reshape(1, -1), lam_k1.reshape(1, -1), lam_q2.reshape(1, -1), lam_k2.reshape(1, -1),
      subln.reshape(1, -1), qb, kb_new, vb_new, *([cache_b_k] * pps), *([cache_b_v] * pps))


def _split_w_in(w):
    src = {}
    off = 0
    for name, width in SRC_SEGMENTS:
        src[name] = w[:, off:off + width]
        off += width
    cat = lambda *names: jnp.concatenate([src[n] for n in names], axis=1).astype(BF16)
    pad = jnp.zeros((w.shape[0], LANES - IDX_DIM - IDX_HEADS - GLA_RANK), w.dtype)
    return {
        'q128': cat('qa', 'qb'), 'k128': cat('ka', 'kb'), 'qi': cat('qi'), 'v': cat('va', 'vb'),
        'c': cat('qc', 'kc', 'vc', 'rc'), 'gate': cat('gate'),
        'misc': jnp.concatenate([src['ki'], src['wi'], src['gc'], pad], axis=1).astype(BF16),
    }


def _rope_tables(pos):
    def tab(dh):
        half = dh // 2
        inv_freq = ROPE_THETA ** (-jnp.arange(half, dtype=F32) / half)
        ang = pos.astype(F32)[:, None] * inv_freq[None, :]
        c, s = jnp.cos(ang), jnp.sin(ang)
        reps = LANES // dh
        return jnp.tile(jnp.concatenate([c, c], axis=1), (1, reps)), jnp.tile(jnp.concatenate([-s, s], axis=1), (1, reps))
    c128, s128 = tab(HEAD_DIM)
    c64, s64 = tab(IDX_DIM)
    n = pos.shape[0]
    tail_c = jnp.concatenate([jnp.full((n, IDX_HEADS), IDX_HEADS ** -0.5, F32),
                              jnp.ones((n, LANES - IDX_DIM - IDX_HEADS), F32)], axis=1)
    cm = jnp.concatenate([c64[:, :IDX_DIM], tail_c], axis=1)
    sm = jnp.concatenate([s64[:, :IDX_DIM], jnp.zeros((n, LANES - IDX_DIM), F32)], axis=1)
    return (c128, s128), (c64, s64), (cm, sm)


BK_ROWS = tuple(range(2 * B_HEADS))
BV_ROWS = tuple((g % 2) * B_HEADS + g // 2 for g in range(2 * B_HEADS))


def _project(x16, wg, tabs, cache=None, layer=0):
    t128, t64, tm_ = tabs
    q128, = proj(x16, wg['q128'], out_dtypes=(BF16,), rope=HEAD_DIM, tables=t128, scale=HEAD_DIM ** -0.5, name='proj_q128')
    qi, = proj(x16, wg['qi'], out_dtypes=(BF16,), rope=IDX_DIM, tables=t64, scale=IDX_DIM ** -0.5, name='proj_qi')
    misc, kz = proj(x16, wg['misc'], out_dtypes=(F32,), rope=IDX_DIM, tables=tm_, emit_kz=True, name='proj_misc')
    cproj, = proj(x16, wg['c'], out_dtypes=(F32,), name='proj_c')
    out = dict(q128=q128, qi=qi, misc=misc, kz=kz, c=cproj)
    if cache is None:
        out['k128f'], out['k128'] = proj(x16, wg['k128'], out_dtypes=(F32, BF16), rope=HEAD_DIM, tables=t128, tn=1280,
                                         name='proj_k128')
        out['vf'], out['v16'] = proj(x16, wg['v'], out_dtypes=(F32, BF16), tn=1280, name='proj_v')
        return out, None
    depth = cache['depth']
    out['k128'], (ak, bk) = proj_cache(x16, wg['k128'], (cache['a_k'], cache['b_k']), layer, depth, b_rows=BK_ROWS,
                                       rope=HEAD_DIM, tables=t128, name='proj_k128c')
    out['v16'], (av, bv) = proj_cache(x16, wg['v'], (cache['a_v'], cache['b_v']), layer, depth, b_rows=BV_ROWS,
                                      name='proj_vc')
    return out, dict(a_k=ak, b_k=bk, a_v=av, b_v=bv, depth=depth)


def _tail(x, x16, oA, oB, oC, lw, alpha):
    g = branch_merge(x16, oA, oB, oC, lw['w_branch'], lw['w_gate'], tm=1024, tn=512)
    x1, x1b = matmul_ln(g, lw['w_out'], x, lw['ln1_g'], lw['ln1_b'], alpha=alpha, tm=512, tk=2048, name='out_ln1')
    hid = mlp_up(x1b, lw['w_up'], tm=1024, tn=1024)
    return matmul_ln(hid, lw['w_down'], x1, lw['ln2_g'], lw['ln2_b'], alpha=alpha, tm=512, tk=2048, name='down_ln2')


def kernel(x_prompt, x_sample, cache_a_k, cache_a_v, cache_idx_k, cache_b_k, cache_b_v, state_gla, page_table,
           w_in, gla_w2, gla_b, lam_q1, lam_k1, lam_q2, lam_k2, diff_subln, gla_norm, w_branch, w_out,
           ln1_g, ln1_b, w_up, w_down, ln2_g, ln2_b):
    B, T, D = x_prompt.shape
    DB, Ts, _ = x_sample.shape
    assert Ts == 1
    DEPTH = w_in.shape[0]
    n_pages = page_table.shape[1]
    P = n_pages * PAGE_SIZE
    alpha = (2.0 * DEPTH) ** 0.25
    n_phys = cache_a_k.shape[1]
    ca_k = cache_a_k.reshape(DEPTH, n_phys, PAGE_SIZE * A_KV_HEADS, HEAD_DIM)
    ca_v = cache_a_v.reshape(DEPTH, n_phys, PAGE_SIZE * A_KV_HEADS, HEAD_DIM)
    cb_k = cache_b_k.reshape(DEPTH, n_phys, PAGE_SIZE * B_HEADS * 2, HEAD_DIM)
    cb_v = jnp.transpose(cache_b_v.reshape(DEPTH, n_phys, PAGE_SIZE, B_HEADS, B_V_DIM // LANES, LANES),
                         (0, 1, 2, 4, 3, 5)).reshape(DEPTH, n_phys, PAGE_SIZE * B_HEADS * 2, LANES)
    ci_kt = jnp.swapaxes(cache_idx_k, 2, 3)
    tabs_p = _rope_tables(jnp.tile(jnp.arange(T), B))
    tabs_s = _rope_tables(jnp.tile(P + jnp.arange(Ts), DB))
    TS_PAD = GLA_CHUNK
    AKV = A_KV_HEADS * HEAD_DIM

    xp = x_prompt.reshape(B * T, D)
    xs = x_sample.reshape(DB * Ts, D)
    xp16, xs16 = xp.astype(BF16), xs.astype(BF16)
    outs = {k: [] for k in ('a_k_s', 'a_v_s', 'i_k_p', 'i_k_s', 'b_k_s', 'b_v_s', 'g_p', 'g_s')}
    cache_p = {'a_k': jnp.zeros((DEPTH * B * T * A_KV_HEADS, LANES), F32),
               'a_v': jnp.zeros((DEPTH * B * T * A_KV_HEADS, LANES), F32),
               'b_k': jnp.zeros((DEPTH * B * T * 2 * B_HEADS, LANES), F32),
               'b_v': jnp.zeros((DEPTH * B * T * 2 * B_HEADS, LANES), F32), 'depth': DEPTH}
    for l in range(DEPTH):
        lam_init = 0.8 - 0.6 * math.exp(-0.3 * l)
        wg = _split_w_in(w_in[l])
        lw = {'w_branch': w_branch[l].astype(BF16), 'w_out': w_out[l].astype(BF16), 'w_up': w_up[l].astype(BF16),
              'w_down': w_down[l].astype(BF16), 'w_gate': wg['gate'], 'ln1_g': ln1_g[l], 'ln1_b': ln1_b[l],
              'ln2_g': ln2_g[l], 'ln2_b': ln2_b[l]}
        w2pad = jnp.zeros((LANES, C_HEADS * C_K_DIM), F32).at[MISC_GC:MISC_GC + GLA_RANK].set(gla_w2[l]).astype(BF16)
        lam_args = (lam_q1[l], lam_k1[l], lam_q2[l], lam_k2[l], diff_subln[l], lam_init)

        pr, cache_p = _project(xp16, wg, tabs_p, cache_p, l)
        r3 = lambda a: a.reshape(B, T, a.shape[-1])
        oA = dsa_prompt(r3(pr['qi']), r3(pr['misc']), r3(pr['kz']), r3(pr['q128']), r3(pr['k128']), r3(pr['v16']))
        oB = diff_prompt(r3(pr['q128']), r3(pr['k128']), r3(pr['v16']), *lam_args)
        oC, Sp = gla(r3(pr['c']), r3(pr['misc']), w2pad, gla_b[l], gla_norm[l],
                     jnp.zeros((B, C_HEADS, C_V_DIM, C_K_DIM), F32))
        xp, xp16 = _tail(xp, xp16, oA.reshape(B * T, -1), oB.reshape(B * T, -1), oC.reshape(B * T, -1), lw, alpha)
        outs['i_k_p'].append(pr['misc'][:, :IDX_DIM].reshape(B, T, IDX_DIM))
        outs['g_p'].append(jnp.swapaxes(Sp, -1, -2))

        ps, _ = _project(xs16, wg, tabs_s)
        ka_new = ps['k128f'][:, :AKV].reshape(DB, A_KV_HEADS, HEAD_DIM)
        va_new = ps['vf'][:, :AKV].reshape(DB, A_KV_HEADS, HEAD_DIM)
        kb_new = ps['k128f'][:, AKV:].reshape(DB, 2 * B_HEADS, HEAD_DIM)
        vb_new = ps['vf'][:, AKV:].reshape(DB, B_HEADS, B_V_DIM)
        ki_new = ps['misc'][:, :IDX_DIM]
        oA = dsa_sample(l, page_table,
                        ps['qi'].reshape(DB, IDX_HEADS, IDX_DIM),
                        ps['misc'][:, MISC_WI:MISC_WI + IDX_HEADS].reshape(DB, IDX_HEADS, 1),
                        ki_new.reshape(DB, 1, IDX_DIM),
                        ps['q128'][:, :A_HEADS * HEAD_DIM].reshape(DB, A_HEADS, HEAD_DIM),
                        jnp.repeat(ka_new, A_REP, axis=1), jnp.repeat(va_new, A_REP, axis=1),
                        ci_kt, ca_k, ca_v)
        oB = diff_sample(l, page_table, ps['q128'][:, A_HEADS * HEAD_DIM:].reshape(DB, 2 * B_HEADS, HEAD_DIM),
                         kb_new, jnp.repeat(vb_new, 2, axis=1), cb_k, cb_v, *lam_args)
        padt = lambda a: jnp.pad(a.reshape(DB, Ts, -1), ((0, 0), (0, TS_PAD - Ts), (0, 0)))
        oC, Ss = gla(padt(ps['c']), padt(ps['misc']), w2pad, gla_b[l], gla_norm[l],
                     jnp.swapaxes(state_gla[l], -1, -2), t_valid=Ts)
        xs, xs16 = _tail(xs, xs16, oA.reshape(DB, -1).astype(BF16), oB.reshape(DB, -1).astype(BF16),
                         oC[:, :Ts].reshape(DB * Ts, -1), lw, alpha)
        outs['a_k_s'].append(ka_new.reshape(DB, Ts, A_KV_HEADS, HEAD_DIM))
        outs['a_v_s'].append(va_new.reshape(DB, Ts, A_KV_HEADS, HEAD_DIM))
        outs['i_k_s'].append(ki_new.reshape(DB, Ts, IDX_DIM))
        outs['b_k_s'].append(kb_new.reshape(DB, Ts, B_HEADS, 2, HEAD_DIM))
        outs['b_v_s'].append(vb_new.reshape(DB, Ts, B_HEADS, B_V_DIM))
        outs['g_s'].append(jnp.swapaxes(Ss, -1, -2))

    st = {k: jnp.stack(v) for k, v in outs.items()}
    st['a_k_p'] = cache_p['a_k'].reshape(DEPTH, B, T, A_KV_HEADS, HEAD_DIM)
    st['a_v_p'] = cache_p['a_v'].reshape(DEPTH, B, T, A_KV_HEADS, HEAD_DIM)
    st['b_k_p'] = cache_p['b_k'].reshape(DEPTH, B, T, B_HEADS, 2, HEAD_DIM)
    st['b_v_p'] = jnp.transpose(cache_p['b_v'].reshape(DEPTH, B, T, B_V_DIM // LANES, B_HEADS, LANES),
                                (0, 1, 2, 4, 3, 5)).reshape(DEPTH, B, T, B_HEADS, B_V_DIM)
    return (xp.reshape(B, T, D), xs.reshape(DB, Ts, D),
            st['a_k_p'], st['a_k_s'], st['a_v_p'], st['a_v_s'], st['i_k_p'], st['i_k_s'],
            st['b_k_p'], st['b_k_s'], st['b_v_p'], st['b_v_s'], st['g_p'], st['g_s'])
```
